```python
import jax, jax.numpy as jnp
from jax import lax
import numpy as np

D_MODEL = 1024
BATCH = 2
SEQ = 16384
DEPTH = 4

N_MIXERS = 2
N_RET_LAYERS = (DEPTH + 1) // 2
N_GDN_LAYERS = DEPTH // 2

RET_HEADS = 4
RET_QK_DIM = D_MODEL // RET_HEADS
RET_V_DIM = 2 * D_MODEL // RET_HEADS
RET_QK_W = RET_HEADS * RET_QK_DIM
RET_V_W = RET_HEADS * RET_V_DIM
RET_IN_W = 2 * RET_QK_W + 2 * RET_V_W
RET_CHUNK = 128
ROPE_BASE = 10000.0

GDN_HEADS = 8
GDN_K_DIM = D_MODEL // GDN_HEADS
GDN_V_DIM = D_MODEL // GDN_HEADS
GDN_QK_W = GDN_HEADS * GDN_K_DIM
GDN_V_W = GDN_HEADS * GDN_V_DIM
GDN_CONV_CH = 2 * GDN_QK_W + GDN_V_W
GDN_IN_W = GDN_CONV_CH + GDN_V_W + 2 * GDN_HEADS
GDN_CHUNK = 64
CONV_WIDTH = 4

D_FF = 4 * D_MODEL
EPS = 1e-6

kernel_name = "hybrid_retention_gated_deltanet_trunk"


def rms_norm(x, gain):
    xf = x.astype(jnp.float32)
    y = xf * lax.rsqrt(jnp.mean(xf * xf, axis=-1, keepdims=True) + EPS)
    return (y * gain.astype(jnp.float32)).astype(x.dtype)


def rotary(x, pos):
    half = x.shape[-1] // 2
    inv_freq = ROPE_BASE ** (-jnp.arange(half, dtype=jnp.float32) / half)
    ang = pos.astype(jnp.float32)[:, None] * inv_freq[None, :]
    cos = jnp.cos(ang)[None, :, None, :]
    sin = jnp.sin(ang)[None, :, None, :]
    x1, x2 = x[..., :half], x[..., half:]
    return jnp.concatenate([x1 * cos - x2 * sin, x1 * sin + x2 * cos], axis=-1)


def to_chunks(t, c):
    b, T, h, d = t.shape
    return t.reshape(b, T // c, c, h, d).transpose(1, 0, 3, 2, 4)


def from_chunks(t):
    n, b, h, c, d = t.shape
    return t.transpose(1, 0, 3, 2, 4).reshape(b, n * c, h, d)


def retention(h, w_in, gn_gain, w_out):
    b, T, _ = h.shape
    proj = (h @ w_in).astype(jnp.float32)
    q, k, v, g = jnp.split(proj, [RET_QK_W, 2 * RET_QK_W, 2 * RET_QK_W + RET_V_W], axis=-1)
    pos = jnp.arange(T)
    q = rotary(q.reshape(b, T, RET_HEADS, RET_QK_DIM), pos)
    k = rotary(k.reshape(b, T, RET_HEADS, RET_QK_DIM), pos) * (RET_QK_DIM ** -0.5)
    v = v.reshape(b, T, RET_HEADS, RET_V_DIM)

    log_gamma = jnp.log1p(-(2.0 ** (-5.0 - jnp.arange(RET_HEADS, dtype=jnp.float32))))
    idx = jnp.arange(RET_CHUNK, dtype=jnp.float32)
    rel = idx[:, None] - idx[None, :]
    causal = rel >= 0
    intra = jnp.where(causal, jnp.exp(log_gamma[:, None, None] * jnp.where(causal, rel, 0.0)), 0.0)
    q_dec = jnp.exp(log_gamma[:, None] * (idx + 1.0))[..., None]
    k_dec = jnp.exp(log_gamma[:, None] * (RET_CHUNK - 1.0 - idx))[..., None]
    chunk_dec = jnp.exp(log_gamma * RET_CHUNK)[:, None, None]

    def step(S, xs):
        qc, kc, vc = xs
        scores = jnp.einsum('bhcd,bhsd->bhcs', qc, kc) * intra
        out = (jnp.einsum('bhcs,bhse->bhce', scores, vc)
               + jnp.einsum('bhcd,bhde->bhce', qc * q_dec, S))
        S = chunk_dec * S + jnp.einsum('bhsd,bhse->bhde', kc * k_dec, vc)
        return S, out

    S0 = jnp.zeros((b, RET_HEADS, RET_QK_DIM, RET_V_DIM), jnp.float32)
    _, o = lax.scan(step, S0, (to_chunks(q, RET_CHUNK), to_chunks(k, RET_CHUNK), to_chunks(v, RET_CHUNK)))
    o = from_chunks(o)
    mu = jnp.mean(o, axis=-1, keepdims=True)
    var = jnp.mean(jnp.square(o - mu), axis=-1, keepdims=True)
    o = ((o - mu) * lax.rsqrt(var + EPS)).reshape(b, T, RET_V_W) * gn_gain.astype(jnp.float32)
    y = jax.nn.silu(g) * o
    return y.astype(h.dtype) @ w_out


def causal_dwconv(x, w):
    return lax.conv_general_dilated(
        x, w[:, None, :], window_strides=(1,), padding=[(CONV_WIDTH - 1, 0)],
        dimension_numbers=('NWC', 'WIO', 'NWC'), feature_group_count=x.shape[-1])


def l2norm(x):
    return x * lax.rsqrt(jnp.sum(x * x, axis=-1, keepdims=True) + EPS)


def gated_deltanet(h, w_in, conv_w, a_log, dt_bias, norm_gain, w_out):
    b, T, _ = h.shape
    f32 = jnp.float32
    proj = h @ w_in
    qkv, z, beta_raw, a_raw = jnp.split(
        proj, [GDN_CONV_CH, GDN_CONV_CH + GDN_V_W, GDN_CONV_CH + GDN_V_W + GDN_HEADS], axis=-1)
    qkv = jax.nn.silu(causal_dwconv(qkv, conv_w.astype(qkv.dtype))).astype(f32)
    q, k, v = jnp.split(qkv, [GDN_QK_W, 2 * GDN_QK_W], axis=-1)
    q = l2norm(q.reshape(b, T, GDN_HEADS, GDN_K_DIM)) * (GDN_K_DIM ** -0.5)
    k = l2norm(k.reshape(b, T, GDN_HEADS, GDN_K_DIM))
    v = v.reshape(b, T, GDN_HEADS, GDN_V_DIM)
    beta = jax.nn.sigmoid(beta_raw.astype(f32))
    g = -jnp.exp(a_log.astype(f32)) * jax.nn.softplus(a_raw.astype(f32) + dt_bias.astype(f32))

    C = GDN_CHUNK
    qc, kc, vc = to_chunks(q, C), to_chunks(k, C), to_chunks(v, C)
    bc = to_chunks(beta[..., None], C)[..., 0]
    gc = jnp.cumsum(to_chunks(g[..., None], C)[..., 0], axis=-1)
    idx = jnp.arange(C)
    causal = idx[:, None] >= idx[None, :]
    strict = idx[:, None] > idx[None, :]
    diff = gc[..., :, None] - gc[..., None, :]
    decay = jnp.where(causal, jnp.exp(jnp.where(causal, diff, 0.0)), 0.0)

    kb = kc * bc[..., None]
    lower = jnp.where(strict, jnp.einsum('nbhcd,nbhsd->nbhcs', kb, kc) * decay, 0.0)
    a_mat = jnp.eye(C, dtype=f32) + lower
    rhs = jnp.concatenate([vc * bc[..., None], kb * jnp.exp(gc)[..., None]], axis=-1)
    sol = lax.linalg.triangular_solve(a_mat, rhs, left_side=True, lower=True, unit_diagonal=True)
    u, w = sol[..., :GDN_V_DIM], sol[..., GDN_V_DIM:]

    qk = jnp.einsum('nbhcd,nbhsd->nbhcs', qc, kc) * decay
    q_dec = qc * jnp.exp(gc)[..., None]
    g_last = gc[..., -1]
    k_dec = kc * jnp.exp(g_last[..., None] - gc)[..., None]

    def step(S, xs):
        qk_c, q_dec_c, k_dec_c, u_c, w_c, gl = xs
        v_new = u_c - jnp.einsum('bhcd,bhde->bhce', w_c, S)
        out = (jnp.einsum('bhcd,bhde->bhce', q_dec_c, S)
               + jnp.einsum('bhcs,bhse->bhce', qk_c, v_new))
        S = S * jnp.exp(gl)[..., None, None] + jnp.einsum('bhcd,bhce->bhde', k_dec_c, v_new)
        return S, out

    S0 = jnp.zeros((b, GDN_HEADS, GDN_K_DIM, GDN_V_DIM), f32)
    _, o = lax.scan(step, S0, (qk, q_dec, k_dec, u, w, g_last))
    o = from_chunks(o)
    o = o * lax.rsqrt(jnp.mean(o * o, axis=-1, keepdims=True) + EPS) * norm_gain.astype(f32)
    o = o * jax.nn.silu(z.astype(f32).reshape(b, T, GDN_HEADS, GDN_V_DIM))
    return o.reshape(b, T, GDN_V_W).astype(h.dtype) @ w_out


def sq_relu_mlp(h, w_up, w_down):
    return jnp.square(jax.nn.relu(h @ w_up)) @ w_down


def setup_inputs(seed: int = 0) -> dict:
    key = jax.random.key(seed)
    ks = jax.random.split(key, 16)
    nr, nd = N_RET_LAYERS, N_GDN_LAYERS
    nrm = jax.random.normal
    x = nrm(ks[0], (BATCH, SEQ, D_MODEL), jnp.float32)
    norm_gains = 1.0 + 0.02 * nrm(ks[1], (DEPTH, 4, D_MODEL), jnp.float32)
    ret_w_in = nrm(ks[2], (nr, D_MODEL, RET_IN_W), jnp.float32) * D_MODEL ** -0.5
    ret_gn_gain = 1.0 + 0.02 * nrm(ks[3], (nr, RET_V_W), jnp.float32)
    ret_w_out = nrm(ks[4], (nr, RET_V_W, D_MODEL), jnp.float32) * RET_V_W ** -0.5
    gdn_w_in = nrm(ks[5], (nd, D_MODEL, GDN_IN_W), jnp.float32) * D_MODEL ** -0.5
    gdn_conv_w = nrm(ks[6], (nd, CONV_WIDTH, GDN_CONV_CH), jnp.float32) * CONV_WIDTH ** -0.5
    gdn_a_log = jnp.log(jax.random.uniform(ks[7], (nd, GDN_HEADS), jnp.float32, 1.0, 16.0))
    dt = jnp.exp(jax.random.uniform(ks[8], (nd, GDN_HEADS), jnp.float32,
                                    float(np.log(1e-3)), float(np.log(1e-1))))
    gdn_dt_bias = dt + jnp.log(-jnp.expm1(-dt))
    gdn_norm_gain = 1.0 + 0.02 * nrm(ks[9], (nd, GDN_V_DIM), jnp.float32)
    gdn_w_out = nrm(ks[10], (nd, GDN_V_W, D_MODEL), jnp.float32) * GDN_V_W ** -0.5
    mlp_w_up = nrm(ks[11], (DEPTH, D_MODEL, D_FF), jnp.float32) * D_MODEL ** -0.5
    mlp_w_down = nrm(ks[12], (DEPTH, D_FF, D_MODEL), jnp.float32) * D_FF ** -0.5
    return {"x": x, "norm_gains": norm_gains,
            "ret_w_in": ret_w_in, "ret_gn_gain": ret_gn_gain, "ret_w_out": ret_w_out,
            "gdn_w_in": gdn_w_in, "gdn_conv_w": gdn_conv_w, "gdn_a_log": gdn_a_log,
            "gdn_dt_bias": gdn_dt_bias, "gdn_norm_gain": gdn_norm_gain, "gdn_w_out": gdn_w_out,
            "mlp_w_up": mlp_w_up, "mlp_w_down": mlp_w_down}


def reference(x, norm_gains, ret_w_in, ret_gn_gain, ret_w_out, gdn_w_in, gdn_conv_w, gdn_a_log,
              gdn_dt_bias, gdn_norm_gain, gdn_w_out, mlp_w_up, mlp_w_down):
    h = x
    for i in range(DEPTH):
        gains = norm_gains[i]
        j = i // N_MIXERS
        hn = rms_norm(h, gains[0])
        if i % N_MIXERS == 0:
            mix = retention(hn, ret_w_in[j], ret_gn_gain[j], ret_w_out[j])
        else:
            mix = gated_deltanet(hn, gdn_w_in[j], gdn_conv_w[j], gdn_a_log[j], gdn_dt_bias[j],
                                 gdn_norm_gain[j], gdn_w_out[j])
        h = h + rms_norm(mix, gains[1])
        ff = sq_relu_mlp(rms_norm(h, gains[2]), mlp_w_up[i], mlp_w_down[i])
        h = h + rms_norm(ff, gains[3])
    return h
```

```python
import functools

import jax
import jax.numpy as jnp
from jax import lax
from jax.experimental import pallas as pl
from jax.experimental.pallas import tpu as pltpu

F32 = jnp.float32
BF16 = jnp.bfloat16

D_MODEL = 1024
DEPTH = 4
EPS = 1e-6

RET_HEADS = 4
RET_QK_DIM = D_MODEL // RET_HEADS
RET_V_DIM = 2 * D_MODEL // RET_HEADS
RET_QK_W = RET_HEADS * RET_QK_DIM
RET_V_W = RET_HEADS * RET_V_DIM
RET_CHUNK = 128
ROPE_BASE = 10000.0

GDN_HEADS = 8
GDN_DIM = D_MODEL // GDN_HEADS
GDN_W = GDN_HEADS * GDN_DIM
GDN_CHUNK = 64
GDN_BLOCK = 128
CONV_WIDTH = 4
CONV_PAD = 8
GATE_LANES = 128
D_FF = 4 * D_MODEL

PROJ_ROWS = 512
MLP_ROWS = 512
COL_CHUNK = 1024
VMEM_LIMIT = 56 * 1024 * 1024


def _rms(x, gain):
    return x * lax.rsqrt(jnp.mean(x * x, axis=-1, keepdims=True) + EPS) * gain


def _dot(a, b):
    return jnp.dot(a.astype(BF16), b.astype(BF16), preferred_element_type=F32)


def _dot_nt(a, b):
    return lax.dot_general(a.astype(BF16), b.astype(BF16), (((1,), (1,)), ((), ())),
                           preferred_element_type=F32)


def _dot_tn(a, b):
    return lax.dot_general(a.astype(BF16), b.astype(BF16), (((0,), (0,)), ((), ())),
                           preferred_element_type=F32)


def _silu(x):
    return x * jax.nn.sigmoid(x)


def _resident(shape):
    zeros = (0,) * len(shape)
    return pl.BlockSpec(shape, lambda *_: zeros, pipeline_mode=pl.Buffered(1))


def _proj_body(n_chunks, has_aux, x_ref, g_ref, w_ref, *rest):
    if has_aux:
        waux_ref, o_ref, oaux_ref = rest
    else:
        (o_ref,) = rest
    xn = _rms(x_ref[...], g_ref[...]).astype(BF16)
    for c in range(n_chunks):
        sl = slice(c * COL_CHUNK, (c + 1) * COL_CHUNK)
        o_ref[:, sl] = jnp.dot(xn, w_ref[:, sl], preferred_element_type=F32).astype(o_ref.dtype)
    if has_aux:
        oaux_ref[...] = jnp.dot(xn, waux_ref[...], preferred_element_type=F32)


def _project(x, gain, w, w_aux=None):
    n, d = x.shape
    n_out = w.shape[1]
    has_aux = w_aux is not None
    in_specs = [pl.BlockSpec((PROJ_ROWS, d), lambda i: (i, 0)),
                _resident((1, d)), _resident((d, n_out))]
    out_specs = [pl.BlockSpec((PROJ_ROWS, n_out), lambda i: (i, 0))]
    out_shape = [jax.ShapeDtypeStruct((n, n_out), BF16)]
    args = [x, gain.reshape(1, d), w]
    if has_aux:
        in_specs.append(_resident((d, GATE_LANES)))
        out_specs.append(pl.BlockSpec((PROJ_ROWS, GATE_LANES), lambda i: (i, 0)))
        out_shape.append(jax.ShapeDtypeStruct((n, GATE_LANES), F32))
        args.append(w_aux)
    out = pl.pallas_call(
        functools.partial(_proj_body, n_out // COL_CHUNK, has_aux),
        grid=(n // PROJ_ROWS,),
        in_specs=in_specs, out_specs=out_specs, out_shape=out_shape,
        compiler_params=pltpu.CompilerParams(dimension_semantics=("parallel",),
                                             vmem_limit_bytes=VMEM_LIMIT),
        name="norm_project",
    )(*args)
    return out if has_aux else out[0]


def _ret_body(q_ref, k_ref, v_ref, g_ref, cos_ref, sin_ref, intra_ref, qdec_ref, kdec_ref,
              cdec_ref, gn_ref, y_ref, s_ref):
    @pl.when(pl.program_id(1) == 0)
    def _():
        s_ref[...] = jnp.zeros_like(s_ref)

    cos = cos_ref[...]
    sin = sin_ref[...]
    half = RET_QK_DIM // 2
    scale = RET_QK_DIM ** -0.5

    def rotate(ref, base):
        x1 = ref[:, base:base + half].astype(F32)
        x2 = ref[:, base + half:base + 2 * half].astype(F32)
        return jnp.concatenate([x1 * cos - x2 * sin, x1 * sin + x2 * cos], axis=-1)

    for h in range(RET_HEADS):
        vs = slice(h * RET_V_DIM, (h + 1) * RET_V_DIM)
        qr = rotate(q_ref, h * RET_QK_DIM)
        kr = rotate(k_ref, h * RET_QK_DIM) * scale
        v = v_ref[:, vs]
        s = s_ref[h]
        scores = _dot_nt(qr, kr) * intra_ref[h]
        o = _dot(scores, v) + _dot(qr * qdec_ref[h], s)
        s_ref[h] = cdec_ref[h] * s + _dot_tn(kr * kdec_ref[h], v)
        d = o - jnp.mean(o, axis=-1, keepdims=True)
        var = jnp.mean(d * d, axis=-1, keepdims=True)
        on = d * lax.rsqrt(var + EPS) * gn_ref[:, vs]
        y_ref[:, vs] = (_silu(g_ref[:, vs].astype(F32)) * on).astype(y_ref.dtype)


def _retention_tables(seq):
    half = RET_QK_DIM // 2
    inv_freq = ROPE_BASE ** (-jnp.arange(half, dtype=F32) / half)
    ang = jnp.arange(seq).astype(F32)[:, None] * inv_freq[None, :]
    log_gamma = jnp.log1p(-(2.0 ** (-5.0 - jnp.arange(RET_HEADS, dtype=F32))))
    idx = jnp.arange(RET_CHUNK, dtype=F32)
    rel = idx[:, None] - idx[None, :]
    causal = rel >= 0
    intra = jnp.where(causal, jnp.exp(log_gamma[:, None, None] * jnp.where(causal, rel, 0.0)), 0.0)
    q_dec = jnp.exp(log_gamma[:, None] * (idx + 1.0))[..., None]
    k_dec = jnp.exp(log_gamma[:, None] * (RET_CHUNK - 1.0 - idx))[..., None]
    chunk_dec = jnp.exp(log_gamma * RET_CHUNK)[:, None, None]
    shape_qk = (RET_HEADS, RET_CHUNK, RET_QK_DIM)
    return (jnp.cos(ang), jnp.sin(ang), intra,
            jnp.broadcast_to(q_dec, shape_qk), jnp.broadcast_to(k_dec, shape_qk),
            jnp.broadcast_to(chunk_dec, (RET_HEADS, 1, RET_V_DIM)))


def _retention_core(proj, gn_gain, batch, seq):
    c = RET_CHUNK
    nt = seq // c
    cos, sin, intra, q_dec, k_dec, chunk_dec = _retention_tables(seq)
    row = lambda b, t: b * nt + t
    kv = RET_V_W // RET_QK_W
    in_specs = [
        pl.BlockSpec((c, RET_QK_W), lambda b, t: (row(b, t), 0)),
        pl.BlockSpec((c, RET_QK_W), lambda b, t: (row(b, t), 1)),
        pl.BlockSpec((c, RET_V_W), lambda b, t: (row(b, t), 2 // kv)),
        pl.BlockSpec((c, RET_V_W), lambda b, t: (row(b, t), 2 // kv + 1)),
        pl.BlockSpec((c, RET_QK_DIM // 2), lambda b, t: (t, 0)),
        pl.BlockSpec((c, RET_QK_DIM // 2), lambda b, t: (t, 0)),
        _resident(intra.shape), _resident(q_dec.shape), _resident(k_dec.shape),
        _resident(chunk_dec.shape), _resident((1, RET_V_W)),
    ]
    return pl.pallas_call(
        _ret_body,
        grid=(batch, nt),
        in_specs=in_specs,
        out_specs=pl.BlockSpec((c, RET_V_W), lambda b, t: (row(b, t), 0)),
        out_shape=jax.ShapeDtypeStruct((batch * seq, RET_V_W), BF16),
        scratch_shapes=[pltpu.VMEM((RET_HEADS, RET_QK_DIM, RET_V_DIM), F32)],
        compiler_params=pltpu.CompilerParams(dimension_semantics=("parallel", "arbitrary"),
                                             vmem_limit_bytes=VMEM_LIMIT),
        name="retention_core",
    )(proj, proj, proj, proj, cos, sin, intra, q_dec, k_dec, chunk_dec,
      gn_gain.reshape(1, RET_V_W))


def _gdn_body(q_ref, k_ref, v_ref, z_ref, ba_ref, cw_ref, gp_ref, ng_ref, y_ref, s_ref, xbuf_ref):
    tb, c, dh = GDN_BLOCK, GDN_CHUNK, GDN_DIM

    @pl.when(pl.program_id(1) == 0)
    def _():
        s_ref[...] = jnp.zeros_like(s_ref)
        xbuf_ref[0:CONV_PAD, :] = jnp.zeros((CONV_PAD, 3 * GDN_W), F32)

    xbuf_ref[CONV_PAD:, 0:GDN_W] = q_ref[...].astype(F32)
    xbuf_ref[CONV_PAD:, GDN_W:2 * GDN_W] = k_ref[...].astype(F32)
    xbuf_ref[CONV_PAD:, 2 * GDN_W:] = v_ref[...].astype(F32)
    acc = cw_ref[CONV_WIDTH - 1:CONV_WIDTH, :] * xbuf_ref[CONV_PAD:CONV_PAD + tb, :]
    for j in range(CONV_WIDTH - 1):
        lo = CONV_PAD - (CONV_WIDTH - 1) + j
        acc = acc + cw_ref[j:j + 1, :] * xbuf_ref[lo:lo + tb, :]
    xbuf_ref[0:CONV_PAD, :] = xbuf_ref[tb:tb + CONV_PAD, :]
    act = _silu(acc)

    ba = ba_ref[...]
    beta = jax.nn.sigmoid(ba)
    xs = ba + gp_ref[1:2, :]
    softplus = jnp.maximum(xs, 0.0) + jnp.log1p(jnp.exp(-jnp.abs(xs)))
    g = -jnp.exp(gp_ref[0:1, :]) * softplus
    ri = lax.broadcasted_iota(jnp.int32, (tb, tb), 0)
    ci = lax.broadcasted_iota(jnp.int32, (tb, tb), 1)
    same = (ri // c) == (ci // c)
    causal = same & (ri >= ci)
    strict = same & (ri > ci)
    tri = jnp.where(causal, 1.0, 0.0).astype(F32)
    gc = jnp.dot(tri, g, preferred_element_type=F32, precision=lax.Precision.HIGHEST)
    gct = gc.T
    egc = jnp.exp(gc)
    eye = jnp.where(ri == ci, 1.0, 0.0).astype(F32)
    zeros_c = jnp.zeros((c, dh), F32)

    for h in range(GDN_HEADS):
        hs = slice(h * dh, (h + 1) * dh)
        qh = act[:, h * dh:(h + 1) * dh]
        kh = act[:, GDN_W + h * dh:GDN_W + (h + 1) * dh]
        vh = act[:, 2 * GDN_W + h * dh:2 * GDN_W + (h + 1) * dh]
        qh = qh * lax.rsqrt(jnp.sum(qh * qh, axis=-1, keepdims=True) + EPS) * (dh ** -0.5)
        kh = kh * lax.rsqrt(jnp.sum(kh * kh, axis=-1, keepdims=True) + EPS)
        bcol = beta[:, h:h + 1]
        gcol = gc[:, GDN_HEADS + h:GDN_HEADS + h + 1]
        ecol = egc[:, GDN_HEADS + h:GDN_HEADS + h + 1]
        grow = gct[GDN_HEADS + h:GDN_HEADS + h + 1, :]
        decay = jnp.where(causal, jnp.exp(jnp.where(causal, gcol - grow, 0.0)), 0.0)
        kb = kh * bcol
        low = jnp.where(strict, _dot_nt(kb, kh) * decay, 0.0)
        qk = _dot_nt(qh, kh) * decay
        inv = eye - low
        pw = low
        for _ in range(c.bit_length() - 2):
            pw = _dot(pw, pw)
            inv = inv + _dot(inv, pw)
        uw = _dot(inv, jnp.concatenate([vh * bcol, kb * ecol], axis=-1))
        u, w = uw[:, :dh], uw[:, dh:]
        qd = qh * ecol
        s = s_ref[h]
        outs = []
        for j in range(tb // c):
            r = slice(j * c, (j + 1) * c)
            gl = gc[(j + 1) * c - 1:(j + 1) * c, GDN_HEADS + h:GDN_HEADS + h + 1]
            ws = _dot(jnp.concatenate([w[r], qd[r]], axis=0), s)
            vnew = u[r] - ws[:c]
            vpad = jnp.concatenate([zeros_c] * j + [vnew] + [zeros_c] * (tb // c - 1 - j), axis=0)
            outs.append(ws[c:] + _dot(qk[r], vpad))
            s = s * jnp.exp(gl) + _dot_tn(kh[r] * jnp.exp(gl - gcol[r]), vnew)
        s_ref[h] = s
        o = jnp.concatenate(outs, axis=0)
        o = o * lax.rsqrt(jnp.mean(o * o, axis=-1, keepdims=True) + EPS) * ng_ref[...]
        y_ref[:, hs] = (o * _silu(z_ref[:, hs].astype(F32))).astype(y_ref.dtype)


def _gdn_core(proj, ba, conv_w, a_log, dt_bias, norm_gain, batch, seq):
    tb = GDN_BLOCK
    nt = seq // tb
    row = lambda b, t: b * nt + t
    lane_pad = GATE_LANES - 2 * GDN_HEADS
    gate_params = jnp.stack([
        jnp.pad(a_log.astype(F32), (GDN_HEADS, lane_pad)),
        jnp.pad(dt_bias.astype(F32), (GDN_HEADS, lane_pad))])
    gate_params = jnp.pad(gate_params, ((0, 6), (0, 0)))
    in_specs = [
        pl.BlockSpec((tb, GDN_W), lambda b, t: (row(b, t), 0)),
        pl.BlockSpec((tb, GDN_W), lambda b, t: (row(b, t), 1)),
        pl.BlockSpec((tb, GDN_W), lambda b, t: (row(b, t), 2)),
        pl.BlockSpec((tb, GDN_W), lambda b, t: (row(b, t), 3)),
        pl.BlockSpec((tb, GATE_LANES), lambda b, t: (row(b, t), 0)),
        _resident((CONV_WIDTH, 3 * GDN_W)), _resident((8, GATE_LANES)), _resident((1, GDN_DIM)),
    ]
    return pl.pallas_call(
        _gdn_body,
        grid=(batch, nt),
        in_specs=in_specs,
        out_specs=pl.BlockSpec((tb, GDN_W), lambda b, t: (row(b, t), 0)),
        out_shape=jax.ShapeDtypeStruct((batch * seq, GDN_W), BF16),
        scratch_shapes=[pltpu.VMEM((GDN_HEADS, GDN_DIM, GDN_DIM), F32),
                        pltpu.VMEM((CONV_PAD + tb, 3 * GDN_W), F32)],
        compiler_params=pltpu.CompilerParams(dimension_semantics=("parallel", "arbitrary"),
                                             vmem_limit_bytes=VMEM_LIMIT),
        name="gdn_core",
    )(proj, proj, proj, proj, ba, conv_w.astype(F32), gate_params,
      norm_gain.astype(F32).reshape(1, GDN_DIM))


def _out_mlp_body(y_ref, h_ref, wo_ref, gains_ref, wu_ref, wd_ref, o_ref):
    mix = jnp.dot(y_ref[...], wo_ref[...], preferred_element_type=F32)
    h1 = h_ref[...] + _rms(mix, gains_ref[1:2, :])
    xn = _rms(h1, gains_ref[2:3, :]).astype(BF16)
    ff = jnp.zeros_like(h1)
    for c in range(D_FF // COL_CHUNK):
        sl = slice(c * COL_CHUNK, (c + 1) * COL_CHUNK)
        up = jnp.dot(xn, wu_ref[:, sl], preferred_element_type=F32)
        act = jnp.square(jnp.maximum(up, 0.0)).astype(BF16)
        ff = ff + jnp.dot(act, wd_ref[sl, :], preferred_element_type=F32)
    o_ref[...] = h1 + _rms(ff, gains_ref[3:4, :])


def _out_mlp(y, h, w_out, gains, w_up, w_down):
    n, d = h.shape
    k_in = y.shape[1]
    return pl.pallas_call(
        _out_mlp_body,
        grid=(n // MLP_ROWS,),
        in_specs=[pl.BlockSpec((MLP_ROWS, k_in), lambda i: (i, 0)),
                  pl.BlockSpec((MLP_ROWS, d), lambda i: (i, 0)),
                  _resident((k_in, d)), _resident((4, d)),
                  _resident((d, D_FF)), _resident((D_FF, d))],
        out_specs=pl.BlockSpec((MLP_ROWS, d), lambda i: (i, 0)),
        out_shape=jax.ShapeDtypeStruct((n, d), F32),
        compiler_params=pltpu.CompilerParams(dimension_semantics=("parallel",),
                                             vmem_limit_bytes=VMEM_LIMIT),
        name="out_mlp",
    )(y, h, w_out, gains, w_up, w_down)


def kernel(x, norm_gains, ret_w_in, ret_gn_gain, ret_w_out, gdn_w_in, gdn_conv_w, gdn_a_log,
           gdn_dt_bias, gdn_norm_gain, gdn_w_out, mlp_w_up, mlp_w_down):
    batch, seq, d = x.shape
    h = x.reshape(batch * seq, d)
    for i in range(DEPTH):
        gains = norm_gains[i]
        j = i // 2
        if i % 2 == 0:
            proj = _project(h, gains[0], ret_w_in[j].astype(BF16))
            y = _retention_core(proj, ret_gn_gain[j], batch, seq)
            w_out = ret_w_out[j]
        else:
            w_in = gdn_w_in[j]
            n_main = 4 * GDN_W
            w_gate = jnp.pad(w_in[:, n_main:], ((0, 0), (0, GATE_LANES - 2 * GDN_HEADS)))
            proj, ba = _project(h, gains[0], w_in[:, :n_main].astype(BF16), w_gate.astype(BF16))
            y = _gdn_core(proj, ba, gdn_conv_w[j], gdn_a_log[j], gdn_dt_bias[j],
                          gdn_norm_gain[j], batch, seq)
            w_out = gdn_w_out[j]
        h = _out_mlp(y, h, w_out.astype(BF16), gains, mlp_w_up[i].astype(BF16),
                     mlp_w_down[i].astype(BF16))
    return h.reshape(batch, seq, d)
```

```python
import functools

import jax
import jax.numpy as jnp
from jax import lax
from jax.experimental import pallas as pl
from jax.experimental.pallas import tpu as pltpu

F32 = jnp.float32
BF16 = jnp.bfloat16

D_MODEL = 1024
DEPTH = 4
EPS = 1e-6

RET_HEADS = 4
RET_QK_DIM = D_MODEL // RET_HEADS
RET_V_DIM = 2 * D_MODEL // RET_HEADS
RET_QK_W = RET_HEADS * RET_QK_DIM
RET_V_W = RET_HEADS * RET_V_DIM
RET_CHUNK = 128
ROPE_BASE = 10000.0

GDN_HEADS = 8
GDN_DIM = D_MODEL // GDN_HEADS
GDN_W = GDN_HEADS * GDN_DIM
GDN_CHUNK = 64
GDN_BLOCK = 256
GDN_GROUP = 4
CONV_COLS = 512
CONV_WIDTH = 4
CONV_PAD = 8
GATE_LANES = 128
D_FF = 4 * D_MODEL

PROJ_ROWS = 512
MLP_ROWS = 512
COL_CHUNK = 1024
VMEM_LIMIT = 56 * 1024 * 1024


def _rms(x, gain):
    return x * lax.rsqrt(jnp.mean(x * x, axis=-1, keepdims=True) + EPS) * gain


def _dot(a, b):
    return jnp.dot(a.astype(BF16), b.astype(BF16), preferred_element_type=F32)


def _dot_nt(a, b):
    return lax.dot_general(a.astype(BF16), b.astype(BF16), (((1,), (1,)), ((), ())),
                           preferred_element_type=F32)


def _dot_tn(a, b):
    return lax.dot_general(a.astype(BF16), b.astype(BF16), (((0,), (0,)), ((), ())),
                           preferred_element_type=F32)


def _silu(x):
    return x * jax.nn.sigmoid(x)


def _resident(shape):
    zeros = (0,) * len(shape)
    return pl.BlockSpec(shape, lambda *_: zeros, pipeline_mode=pl.Buffered(1))


def _proj_body(n_chunks, has_aux, x_ref, g_ref, w_ref, *rest):
    if has_aux:
        waux_ref, o_ref, oaux_ref = rest
    else:
        (o_ref,) = rest
    xn = _rms(x_ref[...], g_ref[...]).astype(BF16)
    for c in range(n_chunks):
        sl = slice(c * COL_CHUNK, (c + 1) * COL_CHUNK)
        o_ref[:, sl] = jnp.dot(xn, w_ref[:, sl], preferred_element_type=F32).astype(o_ref.dtype)
    if has_aux:
        oaux_ref[...] = jnp.dot(xn, waux_ref[...], preferred_element_type=F32)


def _project(x, gain, w, w_aux=None):
    n, d = x.shape
    n_out = w.shape[1]
    has_aux = w_aux is not None
    in_specs = [pl.BlockSpec((PROJ_ROWS, d), lambda i: (i, 0)),
                _resident((1, d)), _resident((d, n_out))]
    out_specs = [pl.BlockSpec((PROJ_ROWS, n_out), lambda i: (i, 0))]
    out_shape = [jax.ShapeDtypeStruct((n, n_out), BF16)]
    args = [x, gain.reshape(1, d), w]
    if has_aux:
        in_specs.append(_resident((d, GATE_LANES)))
        out_specs.append(pl.BlockSpec((PROJ_ROWS, GATE_LANES), lambda i: (i, 0)))
        out_shape.append(jax.ShapeDtypeStruct((n, GATE_LANES), F32))
        args.append(w_aux)
    out = pl.pallas_call(
        functools.partial(_proj_body, n_out // COL_CHUNK, has_aux),
        grid=(n // PROJ_ROWS,),
        in_specs=in_specs, out_specs=out_specs, out_shape=out_shape,
        compiler_params=pltpu.CompilerParams(dimension_semantics=("parallel",),
                                             vmem_limit_bytes=VMEM_LIMIT),
        name="norm_project",
    )(*args)
    return out if has_aux else out[0]


def _ret_body(q_ref, k_ref, v_ref, g_ref, cos_ref, sin_ref, intra_ref, qdec_ref, kdec_ref,
              cdec_ref, gn_ref, y_ref, s_ref):
    @pl.when(pl.program_id(1) == 0)
    def _():
        s_ref[...] = jnp.zeros_like(s_ref)

    cos = cos_ref[...]
    sin = sin_ref[...]
    half = RET_QK_DIM // 2
    scale = RET_QK_DIM ** -0.5

    def rotate(ref, base):
        x1 = ref[:, base:base + half].astype(F32)
        x2 = ref[:, base + half:base + 2 * half].astype(F32)
        return jnp.concatenate([x1 * cos - x2 * sin, x1 * sin + x2 * cos], axis=-1)

    for h in range(RET_HEADS):
        vs = slice(h * RET_V_DIM, (h + 1) * RET_V_DIM)
        qr = rotate(q_ref, h * RET_QK_DIM)
        kr = rotate(k_ref, h * RET_QK_DIM) * scale
        v = v_ref[:, vs]
        s = s_ref[h]
        scores = _dot_nt(qr, kr) * intra_ref[h]
        o = _dot(scores, v) + _dot(qr * qdec_ref[h], s)
        s_ref[h] = cdec_ref[h] * s + _dot_tn(kr * kdec_ref[h], v)
        d = o - jnp.mean(o, axis=-1, keepdims=True)
        var = jnp.mean(d * d, axis=-1, keepdims=True)
        on = d * lax.rsqrt(var + EPS) * gn_ref[:, vs]
        y_ref[:, vs] = (_silu(g_ref[:, vs].astype(F32)) * on).astype(y_ref.dtype)


def _retention_tables(seq):
    half = RET_QK_DIM // 2
    inv_freq = ROPE_BASE ** (-jnp.arange(half, dtype=F32) / half)
    ang = jnp.arange(seq).astype(F32)[:, None] * inv_freq[None, :]
    log_gamma = jnp.log1p(-(2.0 ** (-5.0 - jnp.arange(RET_HEADS, dtype=F32))))
    idx = jnp.arange(RET_CHUNK, dtype=F32)
    rel = idx[:, None] - idx[None, :]
    causal = rel >= 0
    intra = jnp.where(causal, jnp.exp(log_gamma[:, None, None] * jnp.where(causal, rel, 0.0)), 0.0)
    q_dec = jnp.exp(log_gamma[:, None] * (idx + 1.0))[..., None]
    k_dec = jnp.exp(log_gamma[:, None] * (RET_CHUNK - 1.0 - idx))[..., None]
    chunk_dec = jnp.exp(log_gamma * RET_CHUNK)[:, None, None]
    shape_qk = (RET_HEADS, RET_CHUNK, RET_QK_DIM)
    return (jnp.cos(ang), jnp.sin(ang), intra,
            jnp.broadcast_to(q_dec, shape_qk), jnp.broadcast_to(k_dec, shape_qk),
            jnp.broadcast_to(chunk_dec, (RET_HEADS, 1, RET_V_DIM)))


def _retention_core(proj, gn_gain, batch, seq):
    c = RET_CHUNK
    nt = seq // c
    cos, sin, intra, q_dec, k_dec, chunk_dec = _retention_tables(seq)
    row = lambda b, t: b * nt + t
    kv = RET_V_W // RET_QK_W
    in_specs = [
        pl.BlockSpec((c, RET_QK_W), lambda b, t: (row(b, t), 0)),
        pl.BlockSpec((c, RET_QK_W), lambda b, t: (row(b, t), 1)),
        pl.BlockSpec((c, RET_V_W), lambda b, t: (row(b, t), 2 // kv)),
        pl.BlockSpec((c, RET_V_W), lambda b, t: (row(b, t), 2 // kv + 1)),
        pl.BlockSpec((c, RET_QK_DIM // 2), lambda b, t: (t, 0)),
        pl.BlockSpec((c, RET_QK_DIM // 2), lambda b, t: (t, 0)),
        _resident(intra.shape), _resident(q_dec.shape), _resident(k_dec.shape),
        _resident(chunk_dec.shape), _resident((1, RET_V_W)),
    ]
    return pl.pallas_call(
        _ret_body,
        grid=(batch, nt),
        in_specs=in_specs,
        out_specs=pl.BlockSpec((c, RET_V_W), lambda b, t: (row(b, t), 0)),
        out_shape=jax.ShapeDtypeStruct((batch * seq, RET_V_W), BF16),
        scratch_shapes=[pltpu.VMEM((RET_HEADS, RET_QK_DIM, RET_V_DIM), F32)],
        compiler_params=pltpu.CompilerParams(dimension_semantics=("parallel", "arbitrary"),
                                             vmem_limit_bytes=VMEM_LIMIT),
        name="retention_core",
    )(proj, proj, proj, proj, cos, sin, intra, q_dec, k_dec, chunk_dec,
      gn_gain.reshape(1, RET_V_W))


def _l2norm(x):
    return x * lax.rsqrt(jnp.sum(x * x, axis=-1, keepdims=True) + EPS)


def _gdn_body(q_ref, k_ref, v_ref, z_ref, ba_ref, cw_ref, gp_ref, ng_ref, y_ref,
              s_ref, xbuf_ref, act_ref, qk_ref):
    tb, c, dh, nh = GDN_BLOCK, GDN_CHUNK, GDN_DIM, GDN_HEADS
    nc = tb // c

    @pl.when(pl.program_id(1) == 0)
    def _():
        s_ref[...] = jnp.zeros_like(s_ref)
        xbuf_ref[0:CONV_PAD, :] = jnp.zeros((CONV_PAD, 3 * GDN_W), F32)

    xbuf_ref[CONV_PAD:, 0:GDN_W] = q_ref[...].astype(F32)
    xbuf_ref[CONV_PAD:, GDN_W:2 * GDN_W] = k_ref[...].astype(F32)
    xbuf_ref[CONV_PAD:, 2 * GDN_W:] = v_ref[...].astype(F32)
    for cb in range(3 * GDN_W // CONV_COLS):
        cs = slice(cb * CONV_COLS, (cb + 1) * CONV_COLS)
        acc = cw_ref[CONV_WIDTH - 1:CONV_WIDTH, cs] * xbuf_ref[CONV_PAD:CONV_PAD + tb, cs]
        for j in range(CONV_WIDTH - 1):
            lo = CONV_PAD - (CONV_WIDTH - 1) + j
            acc = acc + cw_ref[j:j + 1, cs] * xbuf_ref[lo:lo + tb, cs]
        act_ref[:, cs] = _silu(acc)
    xbuf_ref[0:CONV_PAD, :] = xbuf_ref[tb:tb + CONV_PAD, :]

    ba = ba_ref[...]
    beta = jax.nn.sigmoid(ba)
    xs = ba + gp_ref[1:2, :]
    softplus = jnp.maximum(xs, 0.0) + jnp.log1p(jnp.exp(-jnp.abs(xs)))
    g = -jnp.exp(gp_ref[0:1, :]) * softplus
    shift = c.bit_length() - 1
    ri = lax.broadcasted_iota(jnp.int32, (tb, tb), 0)
    ci = lax.broadcasted_iota(jnp.int32, (tb, tb), 1)
    same = jnp.right_shift(ri, shift) == jnp.right_shift(ci, shift)
    causal = same & (ri >= ci)
    strict = same & (ri > ci)
    tri = jnp.where(causal, 1.0, 0.0).astype(F32)
    gc = jnp.dot(tri, g, preferred_element_type=F32, precision=lax.Precision.HIGHEST)
    gct = gc.T
    egc = jnp.exp(gc)
    eye_s = jnp.where(lax.broadcasted_iota(jnp.int32, (c, tb), 0)
                      == jnp.bitwise_and(lax.broadcasted_iota(jnp.int32, (c, tb), 1), c - 1),
                      1.0, 0.0).astype(F32)

    def fold(m):
        out = m[0:c]
        for j in range(1, nc):
            out = out + m[j * c:(j + 1) * c]
        return out

    def unfold(m):
        return jnp.where(same, jnp.concatenate([m] * nc, axis=0), 0.0)

    for g0 in range(0, nh, GDN_GROUP):
        heads = list(range(g0, g0 + GDN_GROUP))
        idx = range(len(heads))
        q = [_l2norm(act_ref[:, h * dh:(h + 1) * dh]) * (dh ** -0.5) for h in heads]
        k = [_l2norm(act_ref[:, GDN_W + h * dh:GDN_W + (h + 1) * dh]) for h in heads]
        v = [act_ref[:, 2 * GDN_W + h * dh:2 * GDN_W + (h + 1) * dh] for h in heads]
        bcol = [beta[:, h:h + 1] for h in heads]
        gcol = [gc[:, nh + h:nh + h + 1] for h in heads]
        ecol = [egc[:, nh + h:nh + h + 1] for h in heads]
        grow = [gct[nh + h:nh + h + 1, :] for h in heads]
        kb = [k[i] * bcol[i] for i in idx]
        kq = [_dot_nt(jnp.concatenate([kb[i], q[i]], axis=0), k[i]) for i in idx]
        decay = [jnp.where(causal, jnp.exp(jnp.where(causal, gcol[i] - grow[i], 0.0)), 0.0)
                 for i in idx]
        low = [jnp.where(strict, kq[i][:tb] * decay[i], 0.0) for i in idx]
        for i in idx:
            qk_ref[heads[i]] = (kq[i][tb:] * decay[i]).astype(BF16)
        ps = [fold(low[i]) for i in idx]
        ts = [eye_s - ps[i] for i in idx]
        ps = [_dot(ps[i], low[i]) for i in idx]
        n_levels = shift - 1
        for lvl in range(n_levels):
            wbd = [unfold(ps[i]).astype(BF16) for i in idx]
            if lvl + 1 < n_levels:
                r = [_dot(jnp.concatenate([ts[i], ps[i]], axis=0), wbd[i]) for i in idx]
                ts = [ts[i] + r[i][:c] for i in idx]
                ps = [r[i][c:] for i in idx]
            else:
                ts = [ts[i] + _dot(ts[i], wbd[i]) for i in idx]
        uw = [_dot(unfold(ts[i]), jnp.concatenate([v[i] * bcol[i], kb[i] * ecol[i]], axis=-1))
              for i in idx]
        qd = [q[i] * ecol[i] for i in idx]
        s = [s_ref[h] for h in heads]
        vnew = [[] for _ in idx]
        inter = [[] for _ in idx]
        for j in range(nc):
            r = slice(j * c, (j + 1) * c)
            ws = [_dot(jnp.concatenate([uw[i][r, dh:], qd[i][r]], axis=0), s[i]) for i in idx]
            for i in idx:
                vnew[i].append(uw[i][r, :dh] - ws[i][:c])
                inter[i].append(ws[i][c:])
            gl = [gc[(j + 1) * c - 1:(j + 1) * c, nh + h:nh + h + 1] for h in heads]
            s = [s[i] * jnp.exp(gl[i]) + _dot_tn(k[i][r] * jnp.exp(gl[i] - gcol[i][r]), vnew[i][j])
                 for i in idx]
        for i in idx:
            h = heads[i]
            hs = slice(h * dh, (h + 1) * dh)
            s_ref[h] = s[i]
            o = jnp.concatenate(inter[i], axis=0) + _dot(qk_ref[h], jnp.concatenate(vnew[i], axis=0))
            o = o * lax.rsqrt(jnp.mean(o * o, axis=-1, keepdims=True) + EPS) * ng_ref[...]
            y_ref[:, hs] = (o * _silu(z_ref[:, hs].astype(F32))).astype(y_ref.dtype)


def _gdn_core(proj, ba, conv_w, a_log, dt_bias, norm_gain, batch, seq):
    tb = GDN_BLOCK
    nt = seq // tb
    row = lambda b, t: b * nt + t
    lane_pad = GATE_LANES - 2 * GDN_HEADS
    gate_params = jnp.stack([
        jnp.pad(a_log.astype(F32), (GDN_HEADS, lane_pad)),
        jnp.pad(dt_bias.astype(F32), (GDN_HEADS, lane_pad))])
    gate_params = jnp.pad(gate_params, ((0, 6), (0, 0)))
    in_specs = [
        pl.BlockSpec((tb, GDN_W), lambda b, t: (row(b, t), 0)),
        pl.BlockSpec((tb, GDN_W), lambda b, t: (row(b, t), 1)),
        pl.BlockSpec((tb, GDN_W), lambda b, t: (row(b, t), 2)),
        pl.BlockSpec((tb, GDN_W), lambda b, t: (row(b, t), 3)),
        pl.BlockSpec((tb, GATE_LANES), lambda b, t: (row(b, t), 0)),
        _resident((CONV_WIDTH, 3 * GDN_W)), _resident((8, GATE_LANES)), _resident((1, GDN_DIM)),
    ]
    return pl.pallas_call(
        _gdn_body,
        grid=(batch, nt),
        in_specs=in_specs,
        out_specs=pl.BlockSpec((tb, GDN_W), lambda b, t: (row(b, t), 0)),
        out_shape=jax.ShapeDtypeStruct((batch * seq, GDN_W), BF16),
        scratch_shapes=[pltpu.VMEM((GDN_HEADS, GDN_DIM, GDN_DIM), F32),
                        pltpu.VMEM((CONV_PAD + tb, 3 * GDN_W), F32),
                        pltpu.VMEM((tb, 3 * GDN_W), F32),
                        pltpu.VMEM((GDN_HEADS, tb, tb), BF16)],
        compiler_params=pltpu.CompilerParams(dimension_semantics=("parallel", "arbitrary"),
                                             vmem_limit_bytes=VMEM_LIMIT),
        name="gdn_core",
    )(proj, proj, proj, proj, ba, conv_w.astype(F32), gate_params,
      norm_gain.astype(F32).reshape(1, GDN_DIM))


def _out_mlp_body(y_ref, h_ref, wo_ref, gains_ref, wu_ref, wd_ref, o_ref):
    mix = jnp.dot(y_ref[...], wo_ref[...], preferred_element_type=F32)
    h1 = h_ref[...] + _rms(mix, gains_ref[1:2, :])
    xn = _rms(h1, gains_ref[2:3, :]).astype(BF16)
    ff = jnp.zeros_like(h1)
    for c in range(D_FF // COL_CHUNK):
        sl = slice(c * COL_CHUNK, (c + 1) * COL_CHUNK)
        up = jnp.dot(xn, wu_ref[:, sl], preferred_element_type=F32)
        act = jnp.square(jnp.maximum(up, 0.0)).astype(BF16)
        ff = ff + jnp.dot(act, wd_ref[sl, :], preferred_element_type=F32)
    o_ref[...] = h1 + _rms(ff, gains_ref[3:4, :])


def _out_mlp(y, h, w_out, gains, w_up, w_down):
    n, d = h.shape
    k_in = y.shape[1]
    return pl.pallas_call(
        _out_mlp_body,
        grid=(n // MLP_ROWS,),
        in_specs=[pl.BlockSpec((MLP_ROWS, k_in), lambda i: (i, 0)),
                  pl.BlockSpec((MLP_ROWS, d), lambda i: (i, 0)),
                  _resident((k_in, d)), _resident((4, d)),
                  _resident((d, D_FF)), _resident((D_FF, d))],
        out_specs=pl.BlockSpec((MLP_ROWS, d), lambda i: (i, 0)),
        out_shape=jax.ShapeDtypeStruct((n, d), F32),
        compiler_params=pltpu.CompilerParams(dimension_semantics=("parallel",),
                                             vmem_limit_bytes=VMEM_LIMIT),
        name="out_mlp",
    )(y, h, w_out, gains, w_up, w_down)


def kernel(x, norm_gains, ret_w_in, ret_gn_gain, ret_w_out, gdn_w_in, gdn_conv_w, gdn_a_log,
           gdn_dt_bias, gdn_norm_gain, gdn_w_out, mlp_w_up, mlp_w_down):
    batch, seq, d = x.shape
    h = x.reshape(batch * seq, d)
    for i in range(DEPTH):
        gains = norm_gains[i]
        j = i // 2
        if i % 2 == 0:
            proj = _project(h, gains[0], ret_w_in[j].astype(BF16))
            y = _retention_core(proj, ret_gn_gain[j], batch, seq)
            w_out = ret_w_out[j]
        else:
            w_in = gdn_w_in[j]
            n_main = 4 * GDN_W
            w_gate = jnp.pad(w_in[:, n_main:], ((0, 0), (0, GATE_LANES - 2 * GDN_HEADS)))
            proj, ba = _project(h, gains[0], w_in[:, :n_main].astype(BF16), w_gate.astype(BF16))
            y = _gdn_core(proj, ba, gdn_conv_w[j], gdn_a_log[j], gdn_dt_bias[j],
                          gdn_norm_gain[j], batch, seq)
            w_out = gdn_w_out[j]
        h = _out_mlp(y, h, w_out.astype(BF16), gains, mlp_w_up[i].astype(BF16),
                     mlp_w_down[i].astype(BF16))
    return h.reshape(batch, seq, d)
```

```python
import functools

import jax
import jax.numpy as jnp
from jax import lax
from jax.experimental import pallas as pl
from jax.experimental.pallas import tpu as pltpu

F32 = jnp.float32
BF16 = jnp.bfloat16

D_MODEL = 1024
DEPTH = 4
EPS = 1e-6
LANES = 128
SUBLANES = 8

RET_HEADS = 4
RET_QK_DIM = D_MODEL // RET_HEADS
RET_V_DIM = 2 * D_MODEL // RET_HEADS
RET_QK_W = RET_HEADS * RET_QK_DIM
RET_V_W = RET_HEADS * RET_V_DIM
RET_CHUNK = 256
ROPE_BASE = 10000.0

GDN_HEADS = 8
GDN_DIM = D_MODEL // GDN_HEADS
GDN_W = GDN_HEADS * GDN_DIM
GDN_CHUNK = 64
GDN_BLOCK = 256
GDN_GROUP = 8
CONV_WIDTH = 4
CONV_COLS = 512
GATE_LANES = LANES
D_FF = 4 * D_MODEL

PROJ_ROWS = 512
PROJ_SUB = 256
MLP_ROWS = 512
MLP_SUB = 256
COL_CHUNK = 1024
VMEM_LIMIT = 56 * 1024 * 1024


def _rms(x, gain):
    return x * lax.rsqrt(jnp.mean(x * x, axis=-1, keepdims=True) + EPS) * gain


def _dot(a, b):
    return jnp.dot(a.astype(BF16), b.astype(BF16), preferred_element_type=F32)


def _dot_nt(a, b):
    return lax.dot_general(a.astype(BF16), b.astype(BF16), (((1,), (1,)), ((), ())),
                           preferred_element_type=F32)


def _dot_tn(a, b):
    return lax.dot_general(a.astype(BF16), b.astype(BF16), (((0,), (0,)), ((), ())),
                           preferred_element_type=F32)


def _silu(x):
    half = 0.5 * x
    return half * jnp.tanh(half) + half


def _l2norm(x):
    return x * lax.rsqrt(jnp.sum(x * x, axis=-1, keepdims=True) + EPS)


def _resident(shape):
    zeros = (0,) * len(shape)
    return pl.BlockSpec(shape, lambda *_: zeros, pipeline_mode=pl.Buffered(1))


def _rows(width):
    return pl.BlockSpec((PROJ_ROWS, width), lambda i: (i, 0))


def _ret_proj_body(x_ref, g_ref, w_ref, cos_ref, sin_ref, qdec_ref, kdec_ref,
                   q_ref, k_ref, qd_ref, kd_ref, v_ref, gate_ref):
    half = RET_QK_DIM // 2
    subs = [slice(r * PROJ_SUB, (r + 1) * PROJ_SUB) for r in range(PROJ_ROWS // PROJ_SUB)]
    xn = [_rms(x_ref[rs, :], g_ref[...]).astype(BF16) for rs in subs]

    def project(c):
        return [jnp.dot(a, w_ref[:, c * COL_CHUNK:(c + 1) * COL_CHUNK], preferred_element_type=F32)
                for a in xn]

    def rotary(accs, scale, dec_ref, out_ref, outd_ref):
        for rs, acc in zip(subs, accs):
            cos = cos_ref[rs, :]
            sin = sin_ref[rs, :]
            for h in range(RET_HEADS):
                base = h * RET_QK_DIM
                x1 = acc[:, base:base + half]
                x2 = acc[:, base + half:base + 2 * half]
                r1 = (x1 * cos - x2 * sin) * scale
                r2 = (x1 * sin + x2 * cos) * scale
                dec = dec_ref[h, rs, :]
                out_ref[rs, base:base + half] = r1.astype(BF16)
                out_ref[rs, base + half:base + 2 * half] = r2.astype(BF16)
                outd_ref[rs, base:base + half] = (r1 * dec).astype(BF16)
                outd_ref[rs, base + half:base + 2 * half] = (r2 * dec).astype(BF16)

    rotary(project(0), 1.0, qdec_ref, q_ref, qd_ref)
    rotary(project(1), RET_QK_DIM ** -0.5, kdec_ref, k_ref, kd_ref)
    n_qk = 2 * RET_QK_W // COL_CHUNK
    n_v = RET_V_W // COL_CHUNK
    for c in range(n_v):
        cs = slice(c * COL_CHUNK, (c + 1) * COL_CHUNK)
        for rs, acc in zip(subs, project(n_qk + c)):
            v_ref[rs, cs] = acc.astype(BF16)
        for rs, acc in zip(subs, project(n_qk + n_v + c)):
            gate_ref[rs, cs] = acc.astype(BF16)


def _retention_tables(seq):
    half = RET_QK_DIM // 2
    inv_freq = ROPE_BASE ** (-jnp.arange(half, dtype=F32) / half)
    ang = jnp.arange(seq).astype(F32)[:, None] * inv_freq[None, :]
    log_gamma = jnp.log1p(-(2.0 ** (-5.0 - jnp.arange(RET_HEADS, dtype=F32))))
    idx = jnp.arange(RET_CHUNK, dtype=F32)
    rel = idx[:, None] - idx[None, :]
    causal = rel >= 0
    intra = jnp.where(causal, jnp.exp(log_gamma[:, None, None] * jnp.where(causal, rel, 0.0)), 0.0)
    q_dec = jnp.exp(log_gamma[:, None] * (idx + 1.0))[..., None]
    k_dec = jnp.exp(log_gamma[:, None] * (RET_CHUNK - 1.0 - idx))[..., None]
    chunk_dec = jnp.exp(log_gamma * RET_CHUNK)[:, None, None]
    reps = PROJ_ROWS // RET_CHUNK
    tile = lambda d: jnp.tile(jnp.broadcast_to(d, (RET_HEADS, RET_CHUNK, half)), (1, reps, 1))
    return (jnp.cos(ang), jnp.sin(ang), intra, tile(q_dec), tile(k_dec),
            jnp.broadcast_to(chunk_dec, (RET_HEADS, 1, RET_V_DIM)))


def _ret_project(x, gain, w, cos, sin, q_dec, k_dec, seq):
    n, d = x.shape
    half = RET_QK_DIM // 2
    tiles_per_seq = seq // PROJ_ROWS
    pos = pl.BlockSpec((PROJ_ROWS, half), lambda i: (i % tiles_per_seq, 0))
    widths = (RET_QK_W, RET_QK_W, RET_QK_W, RET_QK_W, RET_V_W, RET_V_W)
    return pl.pallas_call(
        _ret_proj_body,
        grid=(n // PROJ_ROWS,),
        in_specs=[_rows(d), _resident((1, d)), _resident(w.shape), pos, pos,
                  _resident(q_dec.shape), _resident(k_dec.shape)],
        out_specs=[_rows(wd) for wd in widths],
        out_shape=[jax.ShapeDtypeStruct((n, wd), BF16) for wd in widths],
        compiler_params=pltpu.CompilerParams(dimension_semantics=("parallel",),
                                             vmem_limit_bytes=VMEM_LIMIT),
        name="ret_project",
    )(x, gain.reshape(1, d), w, cos, sin, q_dec, k_dec)


def _ret_body(q_ref, k_ref, qd_ref, kd_ref, v_ref, intra_ref, cdec_ref, o_ref, s_ref):
    @pl.when(pl.program_id(1) == 0)
    def _():
        s_ref[...] = jnp.zeros_like(s_ref)

    for h in range(RET_HEADS):
        qs = slice(h * RET_QK_DIM, (h + 1) * RET_QK_DIM)
        vs = slice(h * RET_V_DIM, (h + 1) * RET_V_DIM)
        v = v_ref[:, vs]
        s = s_ref[h]
        scores = _dot_nt(q_ref[:, qs], k_ref[:, qs]) * intra_ref[h]
        o = _dot(scores, v) + _dot(qd_ref[:, qs], s)
        s_ref[h] = cdec_ref[h] * s + _dot_tn(kd_ref[:, qs], v)
        o_ref[:, vs] = o.astype(o_ref.dtype)


def _retention_core(q, k, qd, kd, v, intra, chunk_dec, batch, seq):
    c = RET_CHUNK
    nt = seq // c
    blk = lambda width: pl.BlockSpec((c, width), lambda b, t: (b * nt + t, 0))
    return pl.pallas_call(
        _ret_body,
        grid=(batch, nt),
        in_specs=[blk(RET_QK_W), blk(RET_QK_W), blk(RET_QK_W), blk(RET_QK_W), blk(RET_V_W),
                  _resident(intra.shape), _resident(chunk_dec.shape)],
        out_specs=blk(RET_V_W),
        out_shape=jax.ShapeDtypeStruct((batch * seq, RET_V_W), BF16),
        scratch_shapes=[pltpu.VMEM((RET_HEADS, RET_QK_DIM, RET_V_DIM), F32)],
        compiler_params=pltpu.CompilerParams(dimension_semantics=("parallel", "arbitrary"),
                                             vmem_limit_bytes=VMEM_LIMIT),
        name="retention_core",
    )(q, k, qd, kd, v, intra, chunk_dec)


def _gdn_proj_body(tiles_per_seq, x_ref, g_ref, w_ref, wg_ref, cw_ref,
                   q_ref, k_ref, v_ref, z_ref, ba_ref, xbuf_ref):
    rows, pad, dh = PROJ_ROWS, SUBLANES, GDN_DIM

    @pl.when(pl.program_id(0) % tiles_per_seq == 0)
    def _():
        xbuf_ref[0:pad, :] = jnp.zeros((pad, 3 * GDN_W), F32)

    xn = _rms(x_ref[...], g_ref[...]).astype(BF16)

    def project(part):
        ps = slice(part * GDN_W, (part + 1) * GDN_W)
        xbuf_ref[pad:, ps] = jnp.dot(xn, w_ref[:, ps], preferred_element_type=F32)

    def conv(part, out_ref):
        for cb in range(GDN_W // CONV_COLS):
            lo = part * GDN_W + cb * CONV_COLS
            cs = slice(lo, lo + CONV_COLS)
            full = xbuf_ref[:, cs]
            prev = pltpu.roll(full, 1, axis=0)
            newer = cw_ref[3:4, cs] * full + cw_ref[2:3, cs] * prev
            older = cw_ref[1:2, cs] * full + cw_ref[0:1, cs] * prev
            acc = (newer + pltpu.roll(older, 2, axis=0))[pad:]
            act = _silu(acc)
            for hh in range(CONV_COLS // dh):
                a = act[:, hh * dh:(hh + 1) * dh]
                if part == 0:
                    a = a * (lax.rsqrt(jnp.sum(a * a, axis=-1, keepdims=True) + EPS) * (dh ** -0.5))
                elif part == 1:
                    a = _l2norm(a)
                oc = cb * CONV_COLS + hh * dh
                out_ref[:, oc:oc + dh] = a.astype(BF16)
        ps = slice(part * GDN_W, (part + 1) * GDN_W)
        xbuf_ref[0:pad, ps] = xbuf_ref[rows:rows + pad, ps]

    project(0)
    project(1)
    conv(0, q_ref)
    project(2)
    conv(1, k_ref)
    z_ref[...] = jnp.dot(xn, w_ref[:, 3 * GDN_W:], preferred_element_type=F32).astype(BF16)
    ba_ref[...] = jnp.dot(xn, wg_ref[...], preferred_element_type=F32)
    conv(2, v_ref)


def _gdn_project(x, gain, w, w_gate, conv_w, seq):
    n, d = x.shape
    widths = (GDN_W, GDN_W, GDN_W, GDN_W)
    return pl.pallas_call(
        functools.partial(_gdn_proj_body, seq // PROJ_ROWS),
        grid=(n // PROJ_ROWS,),
        in_specs=[_rows(d), _resident((1, d)), _resident(w.shape), _resident(w_gate.shape),
                  _resident(conv_w.shape)],
        out_specs=[_rows(wd) for wd in widths] + [_rows(GATE_LANES)],
        out_shape=[jax.ShapeDtypeStruct((n, wd), BF16) for wd in widths]
        + [jax.ShapeDtypeStruct((n, GATE_LANES), F32)],
        scratch_shapes=[pltpu.VMEM((SUBLANES + PROJ_ROWS, 3 * GDN_W), F32)],
        compiler_params=pltpu.CompilerParams(dimension_semantics=("arbitrary",),
                                             vmem_limit_bytes=VMEM_LIMIT),
        name="gdn_project",
    )(x, gain.reshape(1, d), w, w_gate, conv_w)


def _gdn_body(q_ref, k_ref, v_ref, ba_ref, gp_ref, o_ref, s_ref, qk_ref):
    tb, c, dh, nh = GDN_BLOCK, GDN_CHUNK, GDN_DIM, GDN_HEADS
    nc = tb // c

    @pl.when(pl.program_id(1) == 0)
    def _():
        s_ref[...] = jnp.zeros_like(s_ref)

    ba = ba_ref[...]
    beta = jax.nn.sigmoid(ba)
    xs = ba + gp_ref[1:2, :]
    softplus = jnp.maximum(xs, 0.0) + jnp.log1p(jnp.exp(-jnp.abs(xs)))
    g = -jnp.exp(gp_ref[0:1, :]) * softplus
    shift = c.bit_length() - 1
    ri = lax.broadcasted_iota(jnp.int32, (tb, tb), 0)
    ci = lax.broadcasted_iota(jnp.int32, (tb, tb), 1)
    same = jnp.right_shift(ri, shift) == jnp.right_shift(ci, shift)
    causal = same & (ri >= ci)
    strict = same & (ri > ci)
    tri = jnp.where(causal, 1.0, 0.0).astype(F32)
    gc = jnp.dot(tri, g, preferred_element_type=F32, precision=lax.Precision.HIGHEST)
    gct = gc.T
    egc = jnp.exp(gc)
    eye_s = jnp.where(lax.broadcasted_iota(jnp.int32, (c, tb), 0)
                      == jnp.bitwise_and(lax.broadcasted_iota(jnp.int32, (c, tb), 1), c - 1),
                      1.0, 0.0).astype(F32)

    def fold(m):
        out = m[0:c]
        for j in range(1, nc):
            out = out + m[j * c:(j + 1) * c]
        return out

    def unfold(m):
        return jnp.where(same, jnp.concatenate([m] * nc, axis=0), 0.0)

    for g0 in range(0, nh, GDN_GROUP):
        heads = list(range(g0, g0 + GDN_GROUP))
        idx = range(len(heads))
        q = [q_ref[:, h * dh:(h + 1) * dh] for h in heads]
        k = [k_ref[:, h * dh:(h + 1) * dh] for h in heads]
        kf = [k[i].astype(F32) for i in idx]
        v = [v_ref[:, h * dh:(h + 1) * dh].astype(F32) for h in heads]
        bcol = [beta[:, h:h + 1] for h in heads]
        gcol = [gc[:, nh + h:nh + h + 1] for h in heads]
        ecol = [egc[:, nh + h:nh + h + 1] for h in heads]
        grow = [gct[nh + h:nh + h + 1, :] for h in heads]
        kb = [kf[i] * bcol[i] for i in idx]
        kq = [_dot_nt(jnp.concatenate([kb[i].astype(BF16), q[i]], axis=0), k[i]) for i in idx]
        decay = [jnp.where(causal, jnp.exp(jnp.where(causal, gcol[i] - grow[i], 0.0)), 0.0)
                 for i in idx]
        low = [jnp.where(strict, kq[i][:tb] * decay[i], 0.0) for i in idx]
        for i in idx:
            qk_ref[heads[i]] = (kq[i][tb:] * decay[i]).astype(BF16)
        ps = [fold(low[i]) for i in idx]
        ts = [eye_s - ps[i] for i in idx]
        ps = [_dot(ps[i], low[i]) for i in idx]
        n_levels = shift - 1
        for lvl in range(n_levels):
            wbd = [unfold(ps[i]).astype(BF16) for i in idx]
            if lvl + 1 < n_levels:
                r = [_dot(jnp.concatenate([ts[i], ps[i]], axis=0), wbd[i]) for i in idx]
                ts = [ts[i] + r[i][:c] for i in idx]
                ps = [r[i][c:] for i in idx]
            else:
                ts = [ts[i] + _dot(ts[i], wbd[i]) for i in idx]
        uw = [_dot(unfold(ts[i]), jnp.concatenate([v[i] * bcol[i], kb[i] * ecol[i]], axis=-1))
              for i in idx]
        qd = [q[i].astype(F32) * ecol[i] for i in idx]
        s = [s_ref[h] for h in heads]
        vnew = [[] for _ in idx]
        inter = [[] for _ in idx]
        for j in range(nc):
            r = slice(j * c, (j + 1) * c)
            ws = [_dot(jnp.concatenate([uw[i][r, dh:], qd[i][r]], axis=0), s[i]) for i in idx]
            for i in idx:
                vnew[i].append(uw[i][r, :dh] - ws[i][:c])
                inter[i].append(ws[i][c:])
            gl = [gc[(j + 1) * c - 1:(j + 1) * c, nh + h:nh + h + 1] for h in heads]
            s = [s[i] * jnp.exp(gl[i]) + _dot_tn(kf[i][r] * jnp.exp(gl[i] - gcol[i][r]), vnew[i][j])
                 for i in idx]
        for i in idx:
            h = heads[i]
            s_ref[h] = s[i]
            o = jnp.concatenate(inter[i], axis=0) + _dot(qk_ref[h], jnp.concatenate(vnew[i], axis=0))
            o_ref[:, h * dh:(h + 1) * dh] = o.astype(o_ref.dtype)


def _gdn_core(q, k, v, ba, a_log, dt_bias, batch, seq):
    tb = GDN_BLOCK
    nt = seq // tb
    lane_pad = GATE_LANES - 2 * GDN_HEADS
    gate_params = jnp.stack([
        jnp.pad(a_log.astype(F32), (GDN_HEADS, lane_pad)),
        jnp.pad(dt_bias.astype(F32), (GDN_HEADS, lane_pad))])
    gate_params = jnp.pad(gate_params, ((0, SUBLANES - 2), (0, 0)))
    blk = lambda width: pl.BlockSpec((tb, width), lambda b, t: (b * nt + t, 0))
    return pl.pallas_call(
        _gdn_body,
        grid=(batch, nt),
        in_specs=[blk(GDN_W), blk(GDN_W), blk(GDN_W), blk(GATE_LANES),
                  _resident((SUBLANES, GATE_LANES))],
        out_specs=blk(GDN_W),
        out_shape=jax.ShapeDtypeStruct((batch * seq, GDN_W), BF16),
        scratch_shapes=[pltpu.VMEM((GDN_HEADS, GDN_DIM, GDN_DIM), F32),
                        pltpu.VMEM((GDN_HEADS, tb, tb), BF16)],
        compiler_params=pltpu.CompilerParams(dimension_semantics=("parallel", "arbitrary"),
                                             vmem_limit_bytes=VMEM_LIMIT),
        name="gdn_core",
    )(q, k, v, ba, gate_params)


def _out_mlp_body(head_dim, center, o_ref, gate_ref, hg_ref, h_ref, wo_ref, gains_ref, wu_ref,
                  wd_ref, out_ref, y_ref):
    subs = [slice(r * MLP_SUB, (r + 1) * MLP_SUB) for r in range(MLP_ROWS // MLP_SUB)]
    for rs in subs:
        for hd in range(o_ref.shape[1] // head_dim):
            hs = slice(hd * head_dim, (hd + 1) * head_dim)
            o = o_ref[rs, hs].astype(F32)
            if center:
                o = o - jnp.mean(o, axis=-1, keepdims=True)
            on = o * lax.rsqrt(jnp.mean(o * o, axis=-1, keepdims=True) + EPS) * hg_ref[:, hs]
            y_ref[rs, hs] = (on * _silu(gate_ref[rs, hs].astype(F32))).astype(BF16)
    mix = [jnp.dot(y_ref[rs, :], wo_ref[...], preferred_element_type=F32) for rs in subs]
    h1 = [h_ref[rs, :] + _rms(m, gains_ref[1:2, :]) for rs, m in zip(subs, mix)]
    xn = [_rms(a, gains_ref[2:3, :]).astype(BF16) for a in h1]
    ff = [jnp.zeros_like(a) for a in h1]
    for c in range(D_FF // COL_CHUNK):
        sl = slice(c * COL_CHUNK, (c + 1) * COL_CHUNK)
        up = [jnp.dot(a, wu_ref[:, sl], preferred_element_type=F32) for a in xn]
        act = [jnp.square(jnp.maximum(u, 0.0)).astype(BF16) for u in up]
        ff = [f + jnp.dot(a, wd_ref[sl, :], preferred_element_type=F32) for f, a in zip(ff, act)]
    for rs, a, f in zip(subs, h1, ff):
        out_ref[rs, :] = a + _rms(f, gains_ref[3:4, :])


def _out_mlp(o, gate, head_gain, head_dim, center, h, w_out, gains, w_up, w_down):
    n, d = h.shape
    k_in = o.shape[1]
    return pl.pallas_call(
        functools.partial(_out_mlp_body, head_dim, center),
        grid=(n // MLP_ROWS,),
        in_specs=[pl.BlockSpec((MLP_ROWS, k_in), lambda i: (i, 0)),
                  pl.BlockSpec((MLP_ROWS, k_in), lambda i: (i, 0)),
                  _resident((1, k_in)),
                  pl.BlockSpec((MLP_ROWS, d), lambda i: (i, 0)),
                  _resident((k_in, d)), _resident((4, d)),
                  _resident((d, D_FF)), _resident((D_FF, d))],
        out_specs=pl.BlockSpec((MLP_ROWS, d), lambda i: (i, 0)),
        out_shape=jax.ShapeDtypeStruct((n, d), F32),
        scratch_shapes=[pltpu.VMEM((MLP_ROWS, k_in), BF16)],
        compiler_params=pltpu.CompilerParams(dimension_semantics=("parallel",),
                                             vmem_limit_bytes=VMEM_LIMIT),
        name="out_mlp",
    )(o, gate, head_gain, h, w_out, gains, w_up, w_down)


def kernel(x, norm_gains, ret_w_in, ret_gn_gain, ret_w_out, gdn_w_in, gdn_conv_w, gdn_a_log,
           gdn_dt_bias, gdn_norm_gain, gdn_w_out, mlp_w_up, mlp_w_down):
    batch, seq, d = x.shape
    h = x.reshape(batch * seq, d)
    cos, sin, intra, q_dec, k_dec, chunk_dec = _retention_tables(seq)
    for i in range(DEPTH):
        gains = norm_gains[i]
        j = i // 2
        if i % 2 == 0:
            q, k, qd, kd, v, gate = _ret_project(h, gains[0], ret_w_in[j].astype(BF16),
                                                 cos, sin, q_dec, k_dec, seq)
            o = _retention_core(q, k, qd, kd, v, intra, chunk_dec, batch, seq)
            head_gain = ret_gn_gain[j].astype(F32).reshape(1, RET_V_W)
            h = _out_mlp(o, gate, head_gain, RET_V_DIM, True, h, ret_w_out[j].astype(BF16),
                         gains, mlp_w_up[i].astype(BF16), mlp_w_down[i].astype(BF16))
        else:
            w_in = gdn_w_in[j]
            n_main = 4 * GDN_W
            w_gate = jnp.pad(w_in[:, n_main:], ((0, 0), (0, GATE_LANES - 2 * GDN_HEADS)))
            q, k, v, z, ba = _gdn_project(h, gains[0], w_in[:, :n_main].astype(BF16),
                                          w_gate.astype(BF16), gdn_conv_w[j].astype(F32), seq)
            o = _gdn_core(q, k, v, ba, gdn_a_log[j], gdn_dt_bias[j], batch, seq)
            head_gain = jnp.tile(gdn_norm_gain[j].astype(F32), GDN_HEADS).reshape(1, GDN_W)
            h = _out_mlp(o, z, head_gain, GDN_DIM, False, h, gdn_w_out[j].astype(BF16),
                         gains, mlp_w_up[i].astype(BF16), mlp_w_down[i].astype(BF16))
    return h.reshape(batch, seq, d)
```

```python
import functools

import jax
import jax.numpy as jnp
from jax import lax
from jax.experimental import pallas as pl
from jax.experimental.pallas import tpu as pltpu

F32 = jnp.float32
BF16 = jnp.bfloat16

D_MODEL = 1024
DEPTH = 4
EPS = 1e-6
LANES = 128
SUBLANES = 8

RET_HEADS = 4
RET_QK_DIM = D_MODEL // RET_HEADS
RET_V_DIM = 2 * D_MODEL // RET_HEADS
RET_QK_W = RET_HEADS * RET_QK_DIM
RET_V_W = RET_HEADS * RET_V_DIM
RET_CHUNK = 256
RET_STEP = 512
ROPE_BASE = 10000.0

GDN_HEADS = 8
GDN_DIM = D_MODEL // GDN_HEADS
GDN_W = GDN_HEADS * GDN_DIM
GDN_CHUNK = 64
GDN_BLOCK = 256
GDN_STEP = 512
GDN_GROUP = 8
CONV_WIDTH = 4
CONV_COLS = 512
GATE_LANES = LANES
D_FF = 4 * D_MODEL

PROJ_ROWS = 512
PROJ_SUB = 256
MLP_ROWS = 512
MLP_SUB = 256
COL_CHUNK = 1024
VMEM_LIMIT = 56 * 1024 * 1024


def _rms(x, gain):
    return x * lax.rsqrt(jnp.mean(x * x, axis=-1, keepdims=True) + EPS) * gain


def _dot(a, b):
    return jnp.dot(a.astype(BF16), b.astype(BF16), preferred_element_type=F32)


def _dot_nt(a, b):
    return lax.dot_general(a.astype(BF16), b.astype(BF16), (((1,), (1,)), ((), ())),
                           preferred_element_type=F32)


def _dot_tn(a, b):
    return lax.dot_general(a.astype(BF16), b.astype(BF16), (((0,), (0,)), ((), ())),
                           preferred_element_type=F32)


def _silu(x):
    half = 0.5 * x
    return half * jnp.tanh(half) + half


def _l2norm(x):
    return x * lax.rsqrt(jnp.sum(x * x, axis=-1, keepdims=True) + EPS)


def _resident(shape):
    zeros = (0,) * len(shape)
    return pl.BlockSpec(shape, lambda *_: zeros, pipeline_mode=pl.Buffered(1))


def _layer(stacked, layer):
    zeros = (0,) * (stacked.ndim - 1)
    return pl.BlockSpec((None,) + stacked.shape[1:], lambda *_: (layer,) + zeros,
                        pipeline_mode=pl.Buffered(1))


def _rows(width):
    return pl.BlockSpec((PROJ_ROWS, width), lambda i: (i, 0))


def _ret_proj_body(x_ref, g_ref, w_ref, base_ref, off_ref, qdec_ref, kdec_ref,
                   q_ref, k_ref, qd_ref, kd_ref, v_ref, gate_ref):
    half = RET_QK_DIM // 2
    subs = [slice(r * PROJ_SUB, (r + 1) * PROJ_SUB) for r in range(PROJ_ROWS // PROJ_SUB)]
    xn = [_rms(x_ref[rs, :], g_ref[...]).astype(BF16) for rs in subs]

    def project(c):
        return [jnp.dot(a, w_ref[:, c * COL_CHUNK:(c + 1) * COL_CHUNK], preferred_element_type=F32)
                for a in xn]

    def rotary(accs, scale, dec_ref, out_ref, outd_ref):
        for rs, acc in zip(subs, accs):
            cos = base_ref[0:1, :] * off_ref[0, rs, :] - base_ref[1:2, :] * off_ref[1, rs, :]
            sin = base_ref[1:2, :] * off_ref[0, rs, :] + base_ref[0:1, :] * off_ref[1, rs, :]
            for h in range(RET_HEADS):
                base = h * RET_QK_DIM
                x1 = acc[:, base:base + half]
                x2 = acc[:, base + half:base + 2 * half]
                r1 = (x1 * cos - x2 * sin) * scale
                r2 = (x1 * sin + x2 * cos) * scale
                dec = dec_ref[h, rs, :]
                out_ref[rs, base:base + half] = r1.astype(BF16)
                out_ref[rs, base + half:base + 2 * half] = r2.astype(BF16)
                outd_ref[rs, base:base + half] = (r1 * dec).astype(BF16)
                outd_ref[rs, base + half:base + 2 * half] = (r2 * dec).astype(BF16)

    rotary(project(0), 1.0, qdec_ref, q_ref, qd_ref)
    rotary(project(1), RET_QK_DIM ** -0.5, kdec_ref, k_ref, kd_ref)
    n_qk = 2 * RET_QK_W // COL_CHUNK
    n_v = RET_V_W // COL_CHUNK
    for c in range(n_v):
        cs = slice(c * COL_CHUNK, (c + 1) * COL_CHUNK)
        for rs, acc in zip(subs, project(n_qk + c)):
            v_ref[rs, cs] = acc.astype(BF16)
        for rs, acc in zip(subs, project(n_qk + n_v + c)):
            gate_ref[rs, cs] = acc.astype(BF16)


def _retention_tables(seq):
    half = RET_QK_DIM // 2
    inv_freq = ROPE_BASE ** (-jnp.arange(half, dtype=F32) / half)
    base = jnp.arange(0, seq, PROJ_ROWS).astype(F32)[:, None] * inv_freq[None, :]
    off = jnp.arange(PROJ_ROWS).astype(F32)[:, None] * inv_freq[None, :]
    log_gamma = jnp.log1p(-(2.0 ** (-5.0 - jnp.arange(RET_HEADS, dtype=F32))))
    idx = jnp.arange(RET_CHUNK, dtype=F32)
    rel = idx[:, None] - idx[None, :]
    causal = rel >= 0
    intra = jnp.where(causal, jnp.exp(log_gamma[:, None, None] * jnp.where(causal, rel, 0.0)), 0.0)
    q_dec = jnp.exp(log_gamma[:, None] * (idx + 1.0))[..., None]
    k_dec = jnp.exp(log_gamma[:, None] * (RET_CHUNK - 1.0 - idx))[..., None]
    chunk_dec = jnp.exp(log_gamma * RET_CHUNK)[:, None, None]
    reps = PROJ_ROWS // RET_CHUNK
    tile = lambda d: jnp.tile(jnp.broadcast_to(d, (RET_HEADS, RET_CHUNK, half)), (1, reps, 1))
    return (jnp.stack([jnp.cos(base), jnp.sin(base)], axis=1), jnp.stack([jnp.cos(off), jnp.sin(off)]),
            intra, tile(q_dec), tile(k_dec),
            jnp.broadcast_to(chunk_dec, (RET_HEADS, 1, RET_V_DIM)))


def _ret_project(x, gain, w, layer, rope_base, rope_off, q_dec, k_dec, seq):
    n, d = x.shape
    half = RET_QK_DIM // 2
    tiles_per_seq = seq // PROJ_ROWS
    base = pl.BlockSpec((None, 2, half), lambda i: (i % tiles_per_seq, 0, 0))
    widths = (RET_QK_W, RET_QK_W, RET_QK_W, RET_QK_W, RET_V_W, RET_V_W)
    return pl.pallas_call(
        _ret_proj_body,
        grid=(n // PROJ_ROWS,),
        in_specs=[_rows(d), _resident((1, d)), _layer(w, layer), base, _resident(rope_off.shape),
                  _resident(q_dec.shape), _resident(k_dec.shape)],
        out_specs=[_rows(wd) for wd in widths],
        out_shape=[jax.ShapeDtypeStruct((n, wd), BF16) for wd in widths],
        compiler_params=pltpu.CompilerParams(dimension_semantics=("parallel",),
                                             vmem_limit_bytes=VMEM_LIMIT),
        name="ret_project",
    )(x, gain.reshape(1, d), w, rope_base, rope_off, q_dec, k_dec)


def _ret_body(q_ref, k_ref, qd_ref, kd_ref, v_ref, intra_ref, cdec_ref, o_ref, s_ref):
    @pl.when(pl.program_id(1) == 0)
    def _():
        s_ref[...] = jnp.zeros_like(s_ref)

    for j in range(RET_STEP // RET_CHUNK):
        rows = slice(j * RET_CHUNK, (j + 1) * RET_CHUNK)
        for h in range(RET_HEADS):
            qs = slice(h * RET_QK_DIM, (h + 1) * RET_QK_DIM)
            vs = slice(h * RET_V_DIM, (h + 1) * RET_V_DIM)
            v = v_ref[rows, vs]
            s = s_ref[h]
            scores = _dot_nt(q_ref[rows, qs], k_ref[rows, qs]) * intra_ref[h]
            o = _dot(scores, v) + _dot(qd_ref[rows, qs], s)
            s_ref[h] = cdec_ref[h] * s + _dot_tn(kd_ref[rows, qs], v)
            o_ref[rows, vs] = o.astype(o_ref.dtype)


def _retention_core(q, k, qd, kd, v, intra, chunk_dec, batch, seq):
    nt = seq // RET_STEP
    blk = lambda width: pl.BlockSpec((RET_STEP, width), lambda b, t: (b * nt + t, 0))
    return pl.pallas_call(
        _ret_body,
        grid=(batch, nt),
        in_specs=[blk(RET_QK_W), blk(RET_QK_W), blk(RET_QK_W), blk(RET_QK_W), blk(RET_V_W),
                  _resident(intra.shape), _resident(chunk_dec.shape)],
        out_specs=blk(RET_V_W),
        out_shape=jax.ShapeDtypeStruct((batch * seq, RET_V_W), BF16),
        scratch_shapes=[pltpu.VMEM((RET_HEADS, RET_QK_DIM, RET_V_DIM), F32)],
        compiler_params=pltpu.CompilerParams(dimension_semantics=("parallel", "arbitrary"),
                                             vmem_limit_bytes=VMEM_LIMIT),
        name="retention_core",
    )(q, k, qd, kd, v, intra, chunk_dec)


def _gdn_proj_body(tiles_per_seq, x_ref, g_ref, w_ref, wg_ref, cw_ref,
                   q_ref, k_ref, v_ref, z_ref, ba_ref, xbuf_ref):
    rows, pad, dh = PROJ_ROWS, SUBLANES, GDN_DIM

    @pl.when(pl.program_id(0) % tiles_per_seq == 0)
    def _():
        xbuf_ref[0:pad, :] = jnp.zeros((pad, 3 * GDN_W), F32)

    xn = _rms(x_ref[...], g_ref[...]).astype(BF16)

    def project(part):
        ps = slice(part * GDN_W, (part + 1) * GDN_W)
        xbuf_ref[pad:, ps] = jnp.dot(xn, w_ref[:, ps], preferred_element_type=F32)

    def conv(part, out_ref):
        for cb in range(GDN_W // CONV_COLS):
            lo = part * GDN_W + cb * CONV_COLS
            cs = slice(lo, lo + CONV_COLS)
            full = xbuf_ref[:, cs]
            prev = pltpu.roll(full, 1, axis=0)
            newer = cw_ref[3:4, cs] * full + cw_ref[2:3, cs] * prev
            older = cw_ref[1:2, cs] * full + cw_ref[0:1, cs] * prev
            acc = (newer + pltpu.roll(older, 2, axis=0))[pad:]
            act = _silu(acc)
            for hh in range(CONV_COLS // dh):
                a = act[:, hh * dh:(hh + 1) * dh]
                if part == 0:
                    a = a * (lax.rsqrt(jnp.sum(a * a, axis=-1, keepdims=True) + EPS) * (dh ** -0.5))
                elif part == 1:
                    a = _l2norm(a)
                oc = cb * CONV_COLS + hh * dh
                out_ref[:, oc:oc + dh] = a.astype(BF16)
        ps = slice(part * GDN_W, (part + 1) * GDN_W)
        xbuf_ref[0:pad, ps] = xbuf_ref[rows:rows + pad, ps]

    project(0)
    project(1)
    conv(0, q_ref)
    project(2)
    conv(1, k_ref)
    z_ref[...] = jnp.dot(xn, w_ref[:, 3 * GDN_W:], preferred_element_type=F32).astype(BF16)
    ba_ref[...] = jnp.dot(xn, wg_ref[...], preferred_element_type=F32)
    conv(2, v_ref)


def _gdn_project(x, gain, w, w_gate, conv_w, layer, seq):
    n, d = x.shape
    widths = (GDN_W, GDN_W, GDN_W, GDN_W)
    return pl.pallas_call(
        functools.partial(_gdn_proj_body, seq // PROJ_ROWS),
        grid=(n // PROJ_ROWS,),
        in_specs=[_rows(d), _resident((1, d)), _layer(w, layer), _layer(w_gate, layer),
                  _layer(conv_w, layer)],
        out_specs=[_rows(wd) for wd in widths] + [_rows(GATE_LANES)],
        out_shape=[jax.ShapeDtypeStruct((n, wd), BF16) for wd in widths]
        + [jax.ShapeDtypeStruct((n, GATE_LANES), F32)],
        scratch_shapes=[pltpu.VMEM((SUBLANES + PROJ_ROWS, 3 * GDN_W), F32)],
        compiler_params=pltpu.CompilerParams(dimension_semantics=("arbitrary",),
                                             vmem_limit_bytes=VMEM_LIMIT),
        name="gdn_project",
    )(x, gain.reshape(1, d), w, w_gate, conv_w)


def _gdn_block(q_ref, k_ref, v_ref, ba_ref, gp_ref, o_ref, s_ref, qk_ref):
    tb, c, dh, nh = GDN_BLOCK, GDN_CHUNK, GDN_DIM, GDN_HEADS
    nc = tb // c

    ba = ba_ref[...]
    beta = jax.nn.sigmoid(ba)
    xs = ba + gp_ref[1:2, :]
    softplus = jnp.maximum(xs, 0.0) + jnp.log1p(jnp.exp(-jnp.abs(xs)))
    g = -jnp.exp(gp_ref[0:1, :]) * softplus
    shift = c.bit_length() - 1
    ri = lax.broadcasted_iota(jnp.int32, (tb, tb), 0)
    ci = lax.broadcasted_iota(jnp.int32, (tb, tb), 1)
    same = jnp.right_shift(ri, shift) == jnp.right_shift(ci, shift)
    causal = same & (ri >= ci)
    strict = same & (ri > ci)
    tri = jnp.where(causal, 1.0, 0.0).astype(F32)
    gc = jnp.dot(tri, g, preferred_element_type=F32, precision=lax.Precision.HIGHEST)
    gct = gc.T
    egc = jnp.exp(gc)
    eye_s = jnp.where(lax.broadcasted_iota(jnp.int32, (c, tb), 0)
                      == jnp.bitwise_and(lax.broadcasted_iota(jnp.int32, (c, tb), 1), c - 1),
                      1.0, 0.0).astype(F32)

    def fold(m):
        out = m[0:c]
        for j in range(1, nc):
            out = out + m[j * c:(j + 1) * c]
        return out

    def unfold(m):
        return jnp.where(same, jnp.concatenate([m] * nc, axis=0), 0.0)

    for g0 in range(0, nh, GDN_GROUP):
        heads = list(range(g0, g0 + GDN_GROUP))
        idx = range(len(heads))
        q = [q_ref[:, h * dh:(h + 1) * dh] for h in heads]
        k = [k_ref[:, h * dh:(h + 1) * dh] for h in heads]
        kf = [k[i].astype(F32) for i in idx]
        v = [v_ref[:, h * dh:(h + 1) * dh].astype(F32) for h in heads]
        bcol = [beta[:, h:h + 1] for h in heads]
        gcol = [gc[:, nh + h:nh + h + 1] for h in heads]
        ecol = [egc[:, nh + h:nh + h + 1] for h in heads]
        grow = [gct[nh + h:nh + h + 1, :] for h in heads]
        kb = [kf[i] * bcol[i] for i in idx]
        kq = [_dot_nt(jnp.concatenate([kb[i].astype(BF16), q[i]], axis=0), k[i]) for i in idx]
        decay = [jnp.where(causal, jnp.exp(jnp.where(causal, gcol[i] - grow[i], 0.0)), 0.0)
                 for i in idx]
        low = [jnp.where(strict, kq[i][:tb] * decay[i], 0.0) for i in idx]
        for i in idx:
            qk_ref[heads[i]] = (kq[i][tb:] * decay[i]).astype(BF16)
        ps = [fold(low[i]) for i in idx]
        ts = [eye_s - ps[i] for i in idx]
        ps = [_dot(ps[i], low[i]) for i in idx]
        n_levels = shift - 1
        for lvl in range(n_levels):
            wbd = [unfold(ps[i]).astype(BF16) for i in idx]
            if lvl + 1 < n_levels:
                r = [_dot(jnp.concatenate([ts[i], ps[i]], axis=0), wbd[i]) for i in idx]
                ts = [ts[i] + r[i][:c] for i in idx]
                ps = [r[i][c:] for i in idx]
            else:
                ts = [ts[i] + _dot(ts[i], wbd[i]) for i in idx]
        uw = [_dot(unfold(ts[i]), jnp.concatenate([v[i] * bcol[i], kb[i] * ecol[i]], axis=-1))
              for i in idx]
        qd = [q[i].astype(F32) * ecol[i] for i in idx]
        s = [s_ref[h] for h in heads]
        vnew = [[] for _ in idx]
        inter = [[] for _ in idx]
        for j in range(nc):
            r = slice(j * c, (j + 1) * c)
            ws = [_dot(jnp.concatenate([uw[i][r, dh:], qd[i][r]], axis=0), s[i]) for i in idx]
            for i in idx:
                vnew[i].append(uw[i][r, :dh] - ws[i][:c])
                inter[i].append(ws[i][c:])
            gl = [gc[(j + 1) * c - 1:(j + 1) * c, nh + h:nh + h + 1] for h in heads]
            s = [s[i] * jnp.exp(gl[i]) + _dot_tn(kf[i][r] * jnp.exp(gl[i] - gcol[i][r]), vnew[i][j])
                 for i in idx]
        for i in idx:
            h = heads[i]
            s_ref[h] = s[i]
            o = jnp.concatenate(inter[i], axis=0) + _dot(qk_ref[h], jnp.concatenate(vnew[i], axis=0))
            o_ref[:, h * dh:(h + 1) * dh] = o.astype(o_ref.dtype)


def _gdn_body(q_ref, k_ref, v_ref, ba_ref, gp_ref, o_ref, s_ref, qk_ref):
    @pl.when(pl.program_id(1) == 0)
    def _():
        s_ref[...] = jnp.zeros_like(s_ref)

    for sb in range(GDN_STEP // GDN_BLOCK):
        rows = pl.ds(sb * GDN_BLOCK, GDN_BLOCK)
        _gdn_block(q_ref.at[rows], k_ref.at[rows], v_ref.at[rows], ba_ref.at[rows], gp_ref,
                   o_ref.at[rows], s_ref, qk_ref.at[sb])


def _gdn_core(q, k, v, ba, a_log, dt_bias, batch, seq):
    tb = GDN_STEP
    nt = seq // tb
    lane_pad = GATE_LANES - 2 * GDN_HEADS
    gate_params = jnp.stack([
        jnp.pad(a_log.astype(F32), (GDN_HEADS, lane_pad)),
        jnp.pad(dt_bias.astype(F32), (GDN_HEADS, lane_pad))])
    gate_params = jnp.pad(gate_params, ((0, SUBLANES - 2), (0, 0)))
    blk = lambda width: pl.BlockSpec((tb, width), lambda b, t: (b * nt + t, 0))
    return pl.pallas_call(
        _gdn_body,
        grid=(batch, nt),
        in_specs=[blk(GDN_W), blk(GDN_W), blk(GDN_W), blk(GATE_LANES),
                  _resident((SUBLANES, GATE_LANES))],
        out_specs=blk(GDN_W),
        out_shape=jax.ShapeDtypeStruct((batch * seq, GDN_W), BF16),
        scratch_shapes=[pltpu.VMEM((GDN_HEADS, GDN_DIM, GDN_DIM), F32),
                        pltpu.VMEM((GDN_STEP // GDN_BLOCK, GDN_HEADS, GDN_BLOCK, GDN_BLOCK), BF16)],
        compiler_params=pltpu.CompilerParams(dimension_semantics=("parallel", "arbitrary"),
                                             vmem_limit_bytes=VMEM_LIMIT),
        name="gdn_core",
    )(q, k, v, ba, gate_params)


def _out_mlp_body(head_dim, center, o_ref, gate_ref, hg_ref, h_ref, wo_ref, gains_ref, wu_ref,
                  wd_ref, out_ref, y_ref):
    subs = [slice(r * MLP_SUB, (r + 1) * MLP_SUB) for r in range(MLP_ROWS // MLP_SUB)]
    for rs in subs:
        for hd in range(o_ref.shape[1] // head_dim):
            hs = slice(hd * head_dim, (hd + 1) * head_dim)
            o = o_ref[rs, hs].astype(F32)
            if center:
                o = o - jnp.mean(o, axis=-1, keepdims=True)
            on = o * lax.rsqrt(jnp.mean(o * o, axis=-1, keepdims=True) + EPS) * hg_ref[:, hs]
            y_ref[rs, hs] = (on * _silu(gate_ref[rs, hs].astype(F32))).astype(BF16)
    mix = [jnp.dot(y_ref[rs, :], wo_ref[...], preferred_element_type=F32) for rs in subs]
    h1 = [h_ref[rs, :] + _rms(m, gains_ref[1:2, :]) for rs, m in zip(subs, mix)]
    xn = [_rms(a, gains_ref[2:3, :]).astype(BF16) for a in h1]
    ff = [jnp.zeros_like(a) for a in h1]
    for c in range(D_FF // COL_CHUNK):
        sl = slice(c * COL_CHUNK, (c + 1) * COL_CHUNK)
        up = [jnp.dot(a, wu_ref[:, sl], preferred_element_type=F32) for a in xn]
        act = [jnp.square(jnp.maximum(u, 0.0)).astype(BF16) for u in up]
        ff = [f + jnp.dot(a, wd_ref[sl, :], preferred_element_type=F32) for f, a in zip(ff, act)]
    for rs, a, f in zip(subs, h1, ff):
        out_ref[rs, :] = a + _rms(f, gains_ref[3:4, :])


def _out_mlp(o, gate, head_gain, head_dim, center, h, w_out, mixer_layer, gains, w_up, w_down, layer):
    n, d = h.shape
    k_in = o.shape[1]
    return pl.pallas_call(
        functools.partial(_out_mlp_body, head_dim, center),
        grid=(n // MLP_ROWS,),
        in_specs=[pl.BlockSpec((MLP_ROWS, k_in), lambda i: (i, 0)),
                  pl.BlockSpec((MLP_ROWS, k_in), lambda i: (i, 0)),
                  _resident((1, k_in)),
                  pl.BlockSpec((MLP_ROWS, d), lambda i: (i, 0)),
                  _layer(w_out, mixer_layer), _layer(gains, layer),
                  _layer(w_up, layer), _layer(w_down, layer)],
        out_specs=pl.BlockSpec((MLP_ROWS, d), lambda i: (i, 0)),
        out_shape=jax.ShapeDtypeStruct((n, d), F32),
        scratch_shapes=[pltpu.VMEM((MLP_ROWS, k_in), BF16)],
        compiler_params=pltpu.CompilerParams(dimension_semantics=("parallel",),
                                             vmem_limit_bytes=VMEM_LIMIT),
        name="out_mlp",
    )(o, gate, head_gain, h, w_out, gains, w_up, w_down)


def kernel(x, norm_gains, ret_w_in, ret_gn_gain, ret_w_out, gdn_w_in, gdn_conv_w, gdn_a_log,
           gdn_dt_bias, gdn_norm_gain, gdn_w_out, mlp_w_up, mlp_w_down):
    batch, seq, d = x.shape
    h = x.reshape(batch * seq, d)
    rope_base, rope_off, intra, q_dec, k_dec, chunk_dec = _retention_tables(seq)
    gains = norm_gains.astype(F32)
    n_main = 4 * GDN_W
    ret_w_in, ret_w_out, gdn_w_out, mlp_w_up, mlp_w_down = (
        w.astype(BF16) for w in (ret_w_in, ret_w_out, gdn_w_out, mlp_w_up, mlp_w_down))
    gdn_w_main = gdn_w_in[:, :, :n_main].astype(BF16)
    gdn_w_gate = jnp.pad(gdn_w_in[:, :, n_main:],
                         ((0, 0), (0, 0), (0, GATE_LANES - 2 * GDN_HEADS))).astype(BF16)
    gdn_conv_w = gdn_conv_w.astype(F32)
    for i in range(DEPTH):
        j = i // 2
        if i % 2 == 0:
            q, k, qd, kd, v, gate = _ret_project(h, gains[i, 0], ret_w_in, j, rope_base, rope_off,
                                                 q_dec, k_dec, seq)
            o = _retention_core(q, k, qd, kd, v, intra, chunk_dec, batch, seq)
            head_gain = ret_gn_gain[j].astype(F32).reshape(1, RET_V_W)
            h = _out_mlp(o, gate, head_gain, RET_V_DIM, True, h, ret_w_out, j, gains,
                         mlp_w_up, mlp_w_down, i)
        else:
            q, k, v, z, ba = _gdn_project(h, gains[i, 0], gdn_w_main, gdn_w_gate, gdn_conv_w, j, seq)
            o = _gdn_core(q, k, v, ba, gdn_a_log[j], gdn_dt_bias[j], batch, seq)
            head_gain = jnp.tile(gdn_norm_gain[j].astype(F32), GDN_HEADS).reshape(1, GDN_W)
            h = _out_mlp(o, z, head_gain, GDN_DIM, False, h, gdn_w_out, j, gains,
                         mlp_w_up, mlp_w_down, i)
    return h.reshape(batch, seq, d)
```

```python
import functools

import jax
import jax.numpy as jnp
from jax import lax
from jax.experimental import pallas as pl
from jax.experimental.pallas import tpu as pltpu

F32 = jnp.float32
BF16 = jnp.bfloat16

D_MODEL = 1024
DEPTH = 4
EPS = 1e-6
LANES = 128
SUBLANES = 8

RET_HEADS = 4
RET_QK_DIM = D_MODEL // RET_HEADS
RET_V_DIM = 2 * D_MODEL // RET_HEADS
RET_QK_W = RET_HEADS * RET_QK_DIM
RET_V_W = RET_HEADS * RET_V_DIM
RET_CHUNK = 256
RET_STEP = 512
ROPE_BASE = 10000.0

GDN_HEADS = 8
GDN_DIM = D_MODEL // GDN_HEADS
GDN_W = GDN_HEADS * GDN_DIM
GDN_CHUNK = 64
GDN_BLOCK = 256
GDN_STEP = 512
GDN_GROUP = 8
CONV_WIDTH = 4
CONV_COLS = 512
GATE_LANES = LANES
D_FF = 4 * D_MODEL

PROJ_ROWS = 512
PROJ_SUB = 256
MLP_ROWS = 512
MLP_SUB = 256
COL_CHUNK = 1024
VMEM_LIMIT = 56 * 1024 * 1024


def _rms(x, gain):
    return x * lax.rsqrt(jnp.mean(x * x, axis=-1, keepdims=True) + EPS) * gain


def _dot(a, b):
    return jnp.dot(a.astype(BF16), b.astype(BF16), preferred_element_type=F32)


def _dot_nt(a, b):
    return lax.dot_general(a.astype(BF16), b.astype(BF16), (((1,), (1,)), ((), ())),
                           preferred_element_type=F32)


def _dot_tn(a, b):
    return lax.dot_general(a.astype(BF16), b.astype(BF16), (((0,), (0,)), ((), ())),
                           preferred_element_type=F32)


def _silu(x):
    half = 0.5 * x
    return half * jnp.tanh(half) + half


def _l2norm(x):
    return x * lax.rsqrt(jnp.sum(x * x, axis=-1, keepdims=True) + EPS)


def _resident(shape):
    zeros = (0,) * len(shape)
    return pl.BlockSpec(shape, lambda *_: zeros, pipeline_mode=pl.Buffered(1))


def _layer(stacked, layer):
    zeros = (0,) * (stacked.ndim - 1)
    return pl.BlockSpec((None,) + stacked.shape[1:], lambda *_: (layer,) + zeros,
                        pipeline_mode=pl.Buffered(1))


def _rows(width):
    return pl.BlockSpec((PROJ_ROWS, width), lambda i: (i, 0))


def _ret_proj_body(x_ref, g_ref, w_ref, base_ref, off_ref, qdec_ref, kdec_ref,
                   q_ref, k_ref, qd_ref, kd_ref, v_ref, gate_ref):
    half = RET_QK_DIM // 2
    subs = [slice(r * PROJ_SUB, (r + 1) * PROJ_SUB) for r in range(PROJ_ROWS // PROJ_SUB)]
    xn = [_rms(x_ref[rs, :], g_ref[...]).astype(BF16) for rs in subs]

    def project(c):
        return [jnp.dot(a, w_ref[:, c * COL_CHUNK:(c + 1) * COL_CHUNK], preferred_element_type=F32)
                for a in xn]

    def rotary(accs, scale, dec_ref, out_ref, outd_ref):
        for rs, acc in zip(subs, accs):
            cos = base_ref[0:1, :] * off_ref[0, rs, :] - base_ref[1:2, :] * off_ref[1, rs, :]
            sin = base_ref[1:2, :] * off_ref[0, rs, :] + base_ref[0:1, :] * off_ref[1, rs, :]
            for h in range(RET_HEADS):
                base = h * RET_QK_DIM
                x1 = acc[:, base:base + half]
                x2 = acc[:, base + half:base + 2 * half]
                r1 = (x1 * cos - x2 * sin) * scale
                r2 = (x1 * sin + x2 * cos) * scale
                dec = dec_ref[h, rs, :]
                out_ref[rs, base:base + half] = r1.astype(BF16)
                out_ref[rs, base + half:base + 2 * half] = r2.astype(BF16)
                outd_ref[rs, base:base + half] = (r1 * dec).astype(BF16)
                outd_ref[rs, base + half:base + 2 * half] = (r2 * dec).astype(BF16)

    rotary(project(0), 1.0, qdec_ref, q_ref, qd_ref)
    rotary(project(1), RET_QK_DIM ** -0.5, kdec_ref, k_ref, kd_ref)
    n_qk = 2 * RET_QK_W // COL_CHUNK
    n_v = RET_V_W // COL_CHUNK
    for c in range(n_v):
        cs = slice(c * COL_CHUNK, (c + 1) * COL_CHUNK)
        for rs, acc in zip(subs, project(n_qk + c)):
            v_ref[rs, cs] = acc.astype(BF16)
        for rs, acc in zip(subs, project(n_qk + n_v + c)):
            gate_ref[rs, cs] = acc.astype(BF16)


def _retention_tables(seq):
    half = RET_QK_DIM // 2
    inv_freq = ROPE_BASE ** (-jnp.arange(half, dtype=F32) / half)
    base = jnp.arange(0, seq, PROJ_ROWS).astype(F32)[:, None] * inv_freq[None, :]
    off = jnp.arange(PROJ_ROWS).astype(F32)[:, None] * inv_freq[None, :]
    log_gamma = jnp.log1p(-(2.0 ** (-5.0 - jnp.arange(RET_HEADS, dtype=F32))))
    idx = jnp.arange(RET_CHUNK, dtype=F32)
    rel = idx[:, None] - idx[None, :]
    causal = rel >= 0
    intra = jnp.where(causal, jnp.exp(log_gamma[:, None, None] * jnp.where(causal, rel, 0.0)), 0.0)
    q_dec = jnp.exp(log_gamma[:, None] * (idx + 1.0))[..., None]
    k_dec = jnp.exp(log_gamma[:, None] * (RET_CHUNK - 1.0 - idx))[..., None]
    chunk_dec = jnp.exp(log_gamma * RET_CHUNK)[:, None, None]
    reps = PROJ_ROWS // RET_CHUNK
    tile = lambda d: jnp.tile(jnp.broadcast_to(d, (RET_HEADS, RET_CHUNK, half)), (1, reps, 1))
    return (jnp.stack([jnp.cos(base), jnp.sin(base)], axis=1), jnp.stack([jnp.cos(off), jnp.sin(off)]),
            intra, tile(q_dec), tile(k_dec),
            jnp.broadcast_to(chunk_dec, (RET_HEADS, 1, RET_V_DIM)))


def _ret_body(q_ref, k_ref, qd_ref, kd_ref, v_ref, intra_ref, cdec_ref, o_ref, s_ref):
    @pl.when(pl.program_id(1) == 0)
    def _():
        s_ref[...] = jnp.zeros_like(s_ref)

    for j in range(RET_STEP // RET_CHUNK):
        rows = slice(j * RET_CHUNK, (j + 1) * RET_CHUNK)
        for h in range(RET_HEADS):
            qs = slice(h * RET_QK_DIM, (h + 1) * RET_QK_DIM)
            vs = slice(h * RET_V_DIM, (h + 1) * RET_V_DIM)
            v = v_ref[rows, vs]
            s = s_ref[h]
            scores = _dot_nt(q_ref[rows, qs], k_ref[rows, qs]) * intra_ref[h]
            o = _dot(scores, v) + _dot(qd_ref[rows, qs], s)
            s_ref[h] = cdec_ref[h] * s + _dot_tn(kd_ref[rows, qs], v)
            o_ref[rows, vs] = o.astype(o_ref.dtype)


def _ret_mixer_body(x_ref, g_ref, w_ref, base_ref, off_ref, qdec_ref, kdec_ref, intra_ref, cdec_ref,
                    o_ref, gate_ref, s_ref, q_ref, k_ref, qd_ref, kd_ref, v_ref):
    _ret_proj_body(x_ref, g_ref, w_ref, base_ref, off_ref, qdec_ref, kdec_ref,
                   q_ref, k_ref, qd_ref, kd_ref, v_ref, gate_ref)
    _ret_body(q_ref, k_ref, qd_ref, kd_ref, v_ref, intra_ref, cdec_ref, o_ref, s_ref)


def _retention_mixer(x, gain, w, layer, rope_base, rope_off, q_dec, k_dec, intra, chunk_dec, batch, seq):
    n, d = x.shape
    half = RET_QK_DIM // 2
    nt = seq // PROJ_ROWS
    blk = lambda width: pl.BlockSpec((PROJ_ROWS, width), lambda b, t: (b * nt + t, 0))
    base = pl.BlockSpec((None, 2, half), lambda b, t: (t, 0, 0))
    qk_scratch = pltpu.VMEM((PROJ_ROWS, RET_QK_W), BF16)
    return pl.pallas_call(
        _ret_mixer_body,
        grid=(batch, nt),
        in_specs=[blk(d), _resident((1, d)), _layer(w, layer), base, _resident(rope_off.shape),
                  _resident(q_dec.shape), _resident(k_dec.shape),
                  _resident(intra.shape), _resident(chunk_dec.shape)],
        out_specs=[blk(RET_V_W), blk(RET_V_W)],
        out_shape=[jax.ShapeDtypeStruct((n, RET_V_W), BF16)] * 2,
        scratch_shapes=[pltpu.VMEM((RET_HEADS, RET_QK_DIM, RET_V_DIM), F32),
                        qk_scratch, qk_scratch, qk_scratch, qk_scratch,
                        pltpu.VMEM((PROJ_ROWS, RET_V_W), BF16)],
        compiler_params=pltpu.CompilerParams(dimension_semantics=("parallel", "arbitrary"),
                                             vmem_limit_bytes=VMEM_LIMIT),
        name="retention_mixer",
    )(x, gain.reshape(1, d), w, rope_base, rope_off, q_dec, k_dec, intra, chunk_dec)


def _gdn_proj_body(tiles_per_seq, x_ref, g_ref, w_ref, wg_ref, cw_ref,
                   q_ref, k_ref, v_ref, z_ref, ba_ref, xbuf_ref):
    rows, pad, dh = PROJ_ROWS, SUBLANES, GDN_DIM

    @pl.when(pl.program_id(0) % tiles_per_seq == 0)
    def _():
        xbuf_ref[0:pad, :] = jnp.zeros((pad, 3 * GDN_W), F32)

    xn = _rms(x_ref[...], g_ref[...]).astype(BF16)

    def project(part):
        ps = slice(part * GDN_W, (part + 1) * GDN_W)
        xbuf_ref[pad:, ps] = jnp.dot(xn, w_ref[:, ps], preferred_element_type=F32)

    def conv(part, out_ref):
        for cb in range(GDN_W // CONV_COLS):
            lo = part * GDN_W + cb * CONV_COLS
            cs = slice(lo, lo + CONV_COLS)
            full = xbuf_ref[:, cs]
            prev = pltpu.roll(full, 1, axis=0)
            newer = cw_ref[3:4, cs] * full + cw_ref[2:3, cs] * prev
            older = cw_ref[1:2, cs] * full + cw_ref[0:1, cs] * prev
            acc = (newer + pltpu.roll(older, 2, axis=0))[pad:]
            act = _silu(acc)
            for hh in range(CONV_COLS // dh):
                a = act[:, hh * dh:(hh + 1) * dh]
                if part == 0:
                    a = a * (lax.rsqrt(jnp.sum(a * a, axis=-1, keepdims=True) + EPS) * (dh ** -0.5))
                elif part == 1:
                    a = _l2norm(a)
                oc = cb * CONV_COLS + hh * dh
                out_ref[:, oc:oc + dh] = a.astype(BF16)
        ps = slice(part * GDN_W, (part + 1) * GDN_W)
        xbuf_ref[0:pad, ps] = xbuf_ref[rows:rows + pad, ps]

    project(0)
    project(1)
    conv(0, q_ref)
    project(2)
    conv(1, k_ref)
    z_ref[...] = jnp.dot(xn, w_ref[:, 3 * GDN_W:], preferred_element_type=F32).astype(BF16)
    ba_ref[...] = jnp.dot(xn, wg_ref[...], preferred_element_type=F32)
    conv(2, v_ref)


def _gdn_project(x, gain, w, w_gate, conv_w, layer, seq):
    n, d = x.shape
    widths = (GDN_W, GDN_W, GDN_W, GDN_W)
    return pl.pallas_call(
        functools.partial(_gdn_proj_body, seq // PROJ_ROWS),
        grid=(n // PROJ_ROWS,),
        in_specs=[_rows(d), _resident((1, d)), _layer(w, layer), _layer(w_gate, layer),
                  _layer(conv_w, layer)],
        out_specs=[_rows(wd) for wd in widths] + [_rows(GATE_LANES)],
        out_shape=[jax.ShapeDtypeStruct((n, wd), BF16) for wd in widths]
        + [jax.ShapeDtypeStruct((n, GATE_LANES), F32)],
        scratch_shapes=[pltpu.VMEM((SUBLANES + PROJ_ROWS, 3 * GDN_W), F32)],
        compiler_params=pltpu.CompilerParams(dimension_semantics=("arbitrary",),
                                             vmem_limit_bytes=VMEM_LIMIT),
        name="gdn_project",
    )(x, gain.reshape(1, d), w, w_gate, conv_w)


def _gdn_block(q_ref, k_ref, v_ref, ba_ref, gp_ref, o_ref, s_ref, qk_ref):
    tb, c, dh, nh = GDN_BLOCK, GDN_CHUNK, GDN_DIM, GDN_HEADS
    nc = tb // c

    ba = ba_ref[...]
    beta = jax.nn.sigmoid(ba)
    xs = ba + gp_ref[1:2, :]
    softplus = jnp.maximum(xs, 0.0) + jnp.log1p(jnp.exp(-jnp.abs(xs)))
    g = -jnp.exp(gp_ref[0:1, :]) * softplus
    shift = c.bit_length() - 1
    ri = lax.broadcasted_iota(jnp.int32, (tb, tb), 0)
    ci = lax.broadcasted_iota(jnp.int32, (tb, tb), 1)
    same = jnp.right_shift(ri, shift) == jnp.right_shift(ci, shift)
    causal = same & (ri >= ci)
    strict = same & (ri > ci)
    tri = jnp.where(causal, 1.0, 0.0).astype(BF16)
    g_hi = g.astype(BF16)
    g_mid = (g - g_hi.astype(F32)).astype(BF16)
    g_lo = (g - g_hi.astype(F32) - g_mid.astype(F32)).astype(BF16)
    gc = sum(jnp.dot(tri, part, preferred_element_type=F32) for part in (g_hi, g_mid, g_lo))
    gct = gc.T
    egc = jnp.exp(gc)
    eye_s = jnp.where(lax.broadcasted_iota(jnp.int32, (c, tb), 0)
                      == jnp.bitwise_and(lax.broadcasted_iota(jnp.int32, (c, tb), 1), c - 1),
                      1.0, 0.0).astype(F32)

    def fold(m):
        out = m[0:c]
        for j in range(1, nc):
            out = out + m[j * c:(j + 1) * c]
        return out

    same_bf = jnp.where(same, 1.0, 0.0).astype(BF16)

    def unfold(m):
        return jnp.concatenate([m.astype(BF16)] * nc, axis=0) * same_bf

    for g0 in range(0, nh, GDN_GROUP):
        heads = list(range(g0, g0 + GDN_GROUP))
        idx = range(len(heads))
        q = [q_ref[:, h * dh:(h + 1) * dh] for h in heads]
        k = [k_ref[:, h * dh:(h + 1) * dh] for h in heads]
        kf = [k[i].astype(F32) for i in idx]
        v = [v_ref[:, h * dh:(h + 1) * dh].astype(F32) for h in heads]
        bcol = [beta[:, h:h + 1] for h in heads]
        gcol = [gc[:, nh + h:nh + h + 1] for h in heads]
        ecol = [egc[:, nh + h:nh + h + 1] for h in heads]
        grow = [gct[nh + h:nh + h + 1, :] for h in heads]
        kb = [kf[i] * bcol[i] for i in idx]
        kq = [_dot_nt(jnp.concatenate([kb[i].astype(BF16), q[i]], axis=0), k[i]) for i in idx]
        decay = [jnp.exp(jnp.minimum(gcol[i] - grow[i], 0.0)) for i in idx]
        low = [jnp.where(strict, kq[i][:tb] * decay[i], 0.0) for i in idx]
        for i in idx:
            qk_ref[heads[i]] = jnp.where(causal, kq[i][tb:] * decay[i], 0.0).astype(BF16)
        ps = [fold(low[i]) for i in idx]
        ts = [eye_s - ps[i] for i in idx]
        ps = [_dot(ps[i], low[i]) for i in idx]
        n_levels = shift - 1
        for lvl in range(n_levels):
            wbd = [unfold(ps[i]) for i in idx]
            if lvl + 1 < n_levels:
                r = [_dot(jnp.concatenate([ts[i], ps[i]], axis=0), wbd[i]) for i in idx]
                ts = [ts[i] + r[i][:c] for i in idx]
                ps = [r[i][c:] for i in idx]
            else:
                ts = [ts[i] + _dot(ts[i], wbd[i]) for i in idx]
        uw = [_dot(unfold(ts[i]), jnp.concatenate([v[i] * bcol[i], kb[i] * ecol[i]], axis=-1))
              for i in idx]
        qd = [q[i].astype(F32) * ecol[i] for i in idx]
        s = [s_ref[h] for h in heads]
        vnew = [[] for _ in idx]
        inter = [[] for _ in idx]
        for j in range(nc):
            r = slice(j * c, (j + 1) * c)
            ws = [_dot(jnp.concatenate([uw[i][r, dh:], qd[i][r]], axis=0), s[i]) for i in idx]
            for i in idx:
                vnew[i].append(uw[i][r, :dh] - ws[i][:c])
                inter[i].append(ws[i][c:])
            gl = [gc[(j + 1) * c - 1:(j + 1) * c, nh + h:nh + h + 1] for h in heads]
            s = [s[i] * jnp.exp(gl[i]) + _dot_tn(kf[i][r] * jnp.exp(gl[i] - gcol[i][r]), vnew[i][j])
                 for i in idx]
        for i in idx:
            h = heads[i]
            s_ref[h] = s[i]
            o = jnp.concatenate(inter[i], axis=0) + _dot(qk_ref[h], jnp.concatenate(vnew[i], axis=0))
            o_ref[:, h * dh:(h + 1) * dh] = o.astype(o_ref.dtype)


def _gdn_body(q_ref, k_ref, v_ref, ba_ref, gp_ref, o_ref, s_ref, qk_ref):
    @pl.when(pl.program_id(1) == 0)
    def _():
        s_ref[...] = jnp.zeros_like(s_ref)

    for sb in range(GDN_STEP // GDN_BLOCK):
        rows = pl.ds(sb * GDN_BLOCK, GDN_BLOCK)
        _gdn_block(q_ref.at[rows], k_ref.at[rows], v_ref.at[rows], ba_ref.at[rows], gp_ref,
                   o_ref.at[rows], s_ref, qk_ref.at[sb])


def _gdn_core(q, k, v, ba, a_log, dt_bias, batch, seq):
    tb = GDN_STEP
    nt = seq // tb
    lane_pad = GATE_LANES - 2 * GDN_HEADS
    gate_params = jnp.stack([
        jnp.pad(a_log.astype(F32), (GDN_HEADS, lane_pad)),
        jnp.pad(dt_bias.astype(F32), (GDN_HEADS, lane_pad))])
    gate_params = jnp.pad(gate_params, ((0, SUBLANES - 2), (0, 0)))
    blk = lambda width: pl.BlockSpec((tb, width), lambda b, t: (b * nt + t, 0))
    return pl.pallas_call(
        _gdn_body,
        grid=(batch, nt),
        in_specs=[blk(GDN_W), blk(GDN_W), blk(GDN_W), blk(GATE_LANES),
                  _resident((SUBLANES, GATE_LANES))],
        out_specs=blk(GDN_W),
        out_shape=jax.ShapeDtypeStruct((batch * seq, GDN_W), BF16),
        scratch_shapes=[pltpu.VMEM((GDN_HEADS, GDN_DIM, GDN_DIM), F32),
                        pltpu.VMEM((GDN_STEP // GDN_BLOCK, GDN_HEADS, GDN_BLOCK, GDN_BLOCK), BF16)],
        compiler_params=pltpu.CompilerParams(dimension_semantics=("parallel", "arbitrary"),
                                             vmem_limit_bytes=VMEM_LIMIT),
        name="gdn_core",
    )(q, k, v, ba, gate_params)


def _out_mlp_body(head_dim, center, o_ref, gate_ref, hg_ref, h_ref, wo_ref, gains_ref, wu_ref,
                  wd_ref, out_ref, y_ref):
    subs = [slice(r * MLP_SUB, (r + 1) * MLP_SUB) for r in range(MLP_ROWS // MLP_SUB)]
    for rs in subs:
        for hd in range(o_ref.shape[1] // head_dim):
            hs = slice(hd * head_dim, (hd + 1) * head_dim)
            o = o_ref[rs, hs].astype(F32)
            if center:
                o = o - jnp.mean(o, axis=-1, keepdims=True)
            on = o * lax.rsqrt(jnp.mean(o * o, axis=-1, keepdims=True) + EPS) * hg_ref[:, hs]
            y_ref[rs, hs] = (on * _silu(gate_ref[rs, hs].astype(F32))).astype(BF16)
    mix = [jnp.dot(y_ref[rs, :], wo_ref[...], preferred_element_type=F32) for rs in subs]
    h1 = [h_ref[rs, :] + _rms(m, gains_ref[1:2, :]) for rs, m in zip(subs, mix)]
    xn = [_rms(a, gains_ref[2:3, :]).astype(BF16) for a in h1]
    ff = [jnp.zeros_like(a) for a in h1]
    for c in range(D_FF // COL_CHUNK):
        sl = slice(c * COL_CHUNK, (c + 1) * COL_CHUNK)
        up = [jnp.dot(a, wu_ref[:, sl], preferred_element_type=F32) for a in xn]
        act = [jnp.square(jnp.maximum(u, 0.0)).astype(BF16) for u in up]
        ff = [f + jnp.dot(a, wd_ref[sl, :], preferred_element_type=F32) for f, a in zip(ff, act)]
    for rs, a, f in zip(subs, h1, ff):
        out_ref[rs, :] = a + _rms(f, gains_ref[3:4, :])


def _out_mlp(o, gate, head_gain, head_dim, center, h, w_out, mixer_layer, gains, w_up, w_down, layer):
    n, d = h.shape
    k_in = o.shape[1]
    return pl.pallas_call(
        functools.partial(_out_mlp_body, head_dim, center),
        grid=(n // MLP_ROWS,),
        in_specs=[pl.BlockSpec((MLP_ROWS, k_in), lambda i: (i, 0)),
                  pl.BlockSpec((MLP_ROWS, k_in), lambda i: (i, 0)),
                  _resident((1, k_in)),
                  pl.BlockSpec((MLP_ROWS, d), lambda i: (i, 0)),
                  _layer(w_out, mixer_layer), _layer(gains, layer),
                  _layer(w_up, layer), _layer(w_down, layer)],
        out_specs=pl.BlockSpec((MLP_ROWS, d), lambda i: (i, 0)),
        out_shape=jax.ShapeDtypeStruct((n, d), F32),
        scratch_shapes=[pltpu.VMEM((MLP_ROWS, k_in), BF16)],
        compiler_params=pltpu.CompilerParams(dimension_semantics=("parallel",),
                                             vmem_limit_bytes=VMEM_LIMIT),
        name="out_mlp",
    )(o, gate, head_gain, h, w_out, gains, w_up, w_down)


def kernel(x, norm_gains, ret_w_in, ret_gn_gain, ret_w_out, gdn_w_in, gdn_conv_w, gdn_a_log,
           gdn_dt_bias, gdn_norm_gain, gdn_w_out, mlp_w_up, mlp_w_down):
    batch, seq, d = x.shape
    h = x.reshape(batch * seq, d)
    rope_base, rope_off, intra, q_dec, k_dec, chunk_dec = _retention_tables(seq)
    gains = norm_gains.astype(F32)
    n_main = 4 * GDN_W
    ret_w_in, ret_w_out, gdn_w_out, mlp_w_up, mlp_w_down = (
        w.astype(BF16) for w in (ret_w_in, ret_w_out, gdn_w_out, mlp_w_up, mlp_w_down))
    gdn_w_main = gdn_w_in[:, :, :n_main].astype(BF16)
    gdn_w_gate = jnp.pad(gdn_w_in[:, :, n_main:],
                         ((0, 0), (0, 0), (0, GATE_LANES - 2 * GDN_HEADS))).astype(BF16)
    gdn_conv_w = gdn_conv_w.astype(F32)
    for i in range(DEPTH):
        j = i // 2
        if i % 2 == 0:
            o, gate = _retention_mixer(h, gains[i, 0], ret_w_in, j, rope_base, rope_off, q_dec, k_dec,
                                       intra, chunk_dec, batch, seq)
            head_gain = ret_gn_gain[j].astype(F32).reshape(1, RET_V_W)
            h = _out_mlp(o, gate, head_gain, RET_V_DIM, True, h, ret_w_out, j, gains,
                         mlp_w_up, mlp_w_down, i)
        else:
            q, k, v, z, ba = _gdn_project(h, gains[i, 0], gdn_w_main, gdn_w_gate, gdn_conv_w, j, seq)
            o = _gdn_core(q, k, v, ba, gdn_a_log[j], gdn_dt_bias[j], batch, seq)
            head_gain = jnp.tile(gdn_norm_gain[j].astype(F32), GDN_HEADS).reshape(1, GDN_W)
            h = _out_mlp(o, z, head_gain, GDN_DIM, False, h, gdn_w_out, j, gains,
                         mlp_w_up, mlp_w_down, i)
    return h.reshape(batch, seq, d)
```

```python
import functools

import jax
import jax.numpy as jnp
from jax import lax
from jax.experimental import pallas as pl
from jax.experimental.pallas import tpu as pltpu

F32 = jnp.float32
BF16 = jnp.bfloat16

D_MODEL = 1024
DEPTH = 4
EPS = 1e-6
LANES = 128
SUBLANES = 8

RET_HEADS = 4
RET_QK_DIM = D_MODEL // RET_HEADS
RET_V_DIM = 2 * D_MODEL // RET_HEADS
RET_QK_W = RET_HEADS * RET_QK_DIM
RET_V_W = RET_HEADS * RET_V_DIM
RET_CHUNK = 256
RET_STEP = 512
ROPE_BASE = 10000.0

GDN_HEADS = 8
GDN_DIM = D_MODEL // GDN_HEADS
GDN_W = GDN_HEADS * GDN_DIM
GDN_CHUNK = 64
GDN_BLOCK = 256
GDN_STEP = 512
GDN_GROUP = 8
CONV_WIDTH = 4
CONV_COLS = 512
GATE_LANES = LANES
D_FF = 4 * D_MODEL

PROJ_ROWS = 512
PROJ_SUB = 256
MLP_ROWS = 512
MLP_SUB = 256
COL_CHUNK = 1024
VMEM_LIMIT = 56 * 1024 * 1024


def _rms(x, gain):
    return x * lax.rsqrt(jnp.mean(x * x, axis=-1, keepdims=True) + EPS) * gain


def _dot(a, b):
    return jnp.dot(a.astype(BF16), b.astype(BF16), preferred_element_type=F32)


def _dot_nt(a, b):
    return lax.dot_general(a.astype(BF16), b.astype(BF16), (((1,), (1,)), ((), ())),
                           preferred_element_type=F32)


def _dot_tn(a, b):
    return lax.dot_general(a.astype(BF16), b.astype(BF16), (((0,), (0,)), ((), ())),
                           preferred_element_type=F32)


def _silu(x):
    half = 0.5 * x
    return half * jnp.tanh(half) + half


def _l2norm(x):
    return x * lax.rsqrt(jnp.sum(x * x, axis=-1, keepdims=True) + EPS)


def _resident(shape):
    zeros = (0,) * len(shape)
    return pl.BlockSpec(shape, lambda *_: zeros, pipeline_mode=pl.Buffered(1))


def _layer(stacked, layer, cols=None):
    shape = stacked.shape[1:] if cols is None else stacked.shape[1:-1] + (cols,)
    zeros = (0,) * len(shape)
    return pl.BlockSpec((None,) + shape, lambda *_: (layer,) + zeros, pipeline_mode=pl.Buffered(1))


def _ret_proj_body(x_ref, g_ref, w_ref, base_ref, off_ref, qdec_ref, kdec_ref,
                   q_ref, k_ref, qd_ref, kd_ref, v_ref, gate_ref):
    half = RET_QK_DIM // 2
    subs = [slice(r * PROJ_SUB, (r + 1) * PROJ_SUB) for r in range(PROJ_ROWS // PROJ_SUB)]
    xn = [_rms(x_ref[rs, :], g_ref[...]).astype(BF16) for rs in subs]

    def project(c):
        return [jnp.dot(a, w_ref[:, c * COL_CHUNK:(c + 1) * COL_CHUNK], preferred_element_type=F32)
                for a in xn]

    def rotary(accs, scale, dec_ref, out_ref, outd_ref):
        for rs, acc in zip(subs, accs):
            cos = base_ref[0:1, :] * off_ref[0, rs, :] - base_ref[1:2, :] * off_ref[1, rs, :]
            sin = base_ref[1:2, :] * off_ref[0, rs, :] + base_ref[0:1, :] * off_ref[1, rs, :]
            for h in range(RET_HEADS):
                base = h * RET_QK_DIM
                x1 = acc[:, base:base + half]
                x2 = acc[:, base + half:base + 2 * half]
                r1 = (x1 * cos - x2 * sin) * scale
                r2 = (x1 * sin + x2 * cos) * scale
                dec = dec_ref[h, rs, :]
                out_ref[rs, base:base + half] = r1.astype(BF16)
                out_ref[rs, base + half:base + 2 * half] = r2.astype(BF16)
                outd_ref[rs, base:base + half] = (r1 * dec).astype(BF16)
                outd_ref[rs, base + half:base + 2 * half] = (r2 * dec).astype(BF16)

    rotary(project(0), 1.0, qdec_ref, q_ref, qd_ref)
    rotary(project(1), RET_QK_DIM ** -0.5, kdec_ref, k_ref, kd_ref)
    n_qk = 2 * RET_QK_W // COL_CHUNK
    n_v = RET_V_W // COL_CHUNK
    for c in range(n_v):
        cs = slice(c * COL_CHUNK, (c + 1) * COL_CHUNK)
        for rs, acc in zip(subs, project(n_qk + c)):
            v_ref[rs, cs] = acc.astype(BF16)
        for rs, acc in zip(subs, project(n_qk + n_v + c)):
            gate_ref[rs, cs] = acc.astype(BF16)


def _retention_tables(seq):
    half = RET_QK_DIM // 2
    inv_freq = ROPE_BASE ** (-jnp.arange(half, dtype=F32) / half)
    base = jnp.arange(0, seq, PROJ_ROWS).astype(F32)[:, None] * inv_freq[None, :]
    off = jnp.arange(PROJ_ROWS).astype(F32)[:, None] * inv_freq[None, :]
    log_gamma = jnp.log1p(-(2.0 ** (-5.0 - jnp.arange(RET_HEADS, dtype=F32))))
    idx = jnp.arange(RET_CHUNK, dtype=F32)
    rel = idx[:, None] - idx[None, :]
    causal = rel >= 0
    intra = jnp.where(causal, jnp.exp(log_gamma[:, None, None] * jnp.where(causal, rel, 0.0)), 0.0)
    q_dec = jnp.exp(log_gamma[:, None] * (idx + 1.0))[..., None]
    k_dec = jnp.exp(log_gamma[:, None] * (RET_CHUNK - 1.0 - idx))[..., None]
    chunk_dec = jnp.exp(log_gamma * RET_CHUNK)[:, None, None]
    reps = PROJ_ROWS // RET_CHUNK
    tile = lambda d: jnp.tile(jnp.broadcast_to(d, (RET_HEADS, RET_CHUNK, half)), (1, reps, 1))
    return (jnp.stack([jnp.cos(base), jnp.sin(base)], axis=1), jnp.stack([jnp.cos(off), jnp.sin(off)]),
            intra, tile(q_dec), tile(k_dec),
            jnp.broadcast_to(chunk_dec, (RET_HEADS, 1, RET_V_DIM)))


def _ret_body(q_ref, k_ref, qd_ref, kd_ref, v_ref, intra_ref, cdec_ref, o_ref, s_ref):
    @pl.when(pl.program_id(1) == 0)
    def _():
        s_ref[...] = jnp.zeros_like(s_ref)

    for j in range(RET_STEP // RET_CHUNK):
        rows = slice(j * RET_CHUNK, (j + 1) * RET_CHUNK)
        for h in range(RET_HEADS):
            qs = slice(h * RET_QK_DIM, (h + 1) * RET_QK_DIM)
            vs = slice(h * RET_V_DIM, (h + 1) * RET_V_DIM)
            v = v_ref[rows, vs]
            s = s_ref[h]
            scores = _dot_nt(q_ref[rows, qs], k_ref[rows, qs]) * intra_ref[h]
            o = _dot(scores, v) + _dot(qd_ref[rows, qs], s)
            s_ref[h] = cdec_ref[h] * s + _dot_tn(kd_ref[rows, qs], v)
            o_ref[rows, vs] = o.astype(o_ref.dtype)


def _ret_mixer_body(x_ref, g_ref, w_ref, base_ref, off_ref, qdec_ref, kdec_ref, intra_ref, cdec_ref,
                    o_ref, gate_ref, s_ref, q_ref, k_ref, qd_ref, kd_ref, v_ref):
    _ret_proj_body(x_ref, g_ref, w_ref, base_ref, off_ref, qdec_ref, kdec_ref,
                   q_ref, k_ref, qd_ref, kd_ref, v_ref, gate_ref)
    _ret_body(q_ref, k_ref, qd_ref, kd_ref, v_ref, intra_ref, cdec_ref, o_ref, s_ref)


def _retention_mixer(x, gain, w, layer, rope_base, rope_off, q_dec, k_dec, intra, chunk_dec, batch, seq):
    n, d = x.shape
    half = RET_QK_DIM // 2
    nt = seq // PROJ_ROWS
    blk = lambda width: pl.BlockSpec((PROJ_ROWS, width), lambda b, t: (b * nt + t, 0))
    base = pl.BlockSpec((None, 2, half), lambda b, t: (t, 0, 0))
    qk_scratch = pltpu.VMEM((PROJ_ROWS, RET_QK_W), BF16)
    return pl.pallas_call(
        _ret_mixer_body,
        grid=(batch, nt),
        in_specs=[blk(d), _resident((1, d)), _layer(w, layer), base, _resident(rope_off.shape),
                  _resident(q_dec.shape), _resident(k_dec.shape),
                  _resident(intra.shape), _resident(chunk_dec.shape)],
        out_specs=[blk(RET_V_W), blk(RET_V_W)],
        out_shape=[jax.ShapeDtypeStruct((n, RET_V_W), BF16)] * 2,
        scratch_shapes=[pltpu.VMEM((RET_HEADS, RET_QK_DIM, RET_V_DIM), F32),
                        qk_scratch, qk_scratch, qk_scratch, qk_scratch,
                        pltpu.VMEM((PROJ_ROWS, RET_V_W), BF16)],
        compiler_params=pltpu.CompilerParams(dimension_semantics=("parallel", "arbitrary"),
                                             vmem_limit_bytes=VMEM_LIMIT),
        name="retention_mixer",
    )(x, gain.reshape(1, d), w, rope_base, rope_off, q_dec, k_dec, intra, chunk_dec)


def _gdn_proj_body(x_ref, g_ref, w_ref, wg_ref, cw_ref,
                   q_ref, k_ref, v_ref, z_ref, ba_ref, xbuf_ref):
    rows, pad, dh = PROJ_ROWS, SUBLANES, GDN_DIM

    @pl.when(pl.program_id(1) == 0)
    def _():
        xbuf_ref[0:pad, :] = jnp.zeros((pad, 3 * GDN_W), F32)

    xn = _rms(x_ref[...], g_ref[...]).astype(BF16)

    def project(part):
        ps = slice(part * GDN_W, (part + 1) * GDN_W)
        xbuf_ref[pad:, ps] = jnp.dot(xn, w_ref[:, ps], preferred_element_type=F32)

    def conv(part, out_ref):
        for cb in range(GDN_W // CONV_COLS):
            lo = part * GDN_W + cb * CONV_COLS
            cs = slice(lo, lo + CONV_COLS)
            full = xbuf_ref[:, cs]
            prev = pltpu.roll(full, 1, axis=0)
            newer = cw_ref[3:4, cs] * full + cw_ref[2:3, cs] * prev
            older = cw_ref[1:2, cs] * full + cw_ref[0:1, cs] * prev
            acc = (newer + pltpu.roll(older, 2, axis=0))[pad:]
            act = _silu(acc)
            for hh in range(CONV_COLS // dh):
                a = act[:, hh * dh:(hh + 1) * dh]
                if part == 0:
                    a = a * (lax.rsqrt(jnp.sum(a * a, axis=-1, keepdims=True) + EPS) * (dh ** -0.5))
                elif part == 1:
                    a = _l2norm(a)
                oc = cb * CONV_COLS + hh * dh
                out_ref[:, oc:oc + dh] = a.astype(BF16)
        ps = slice(part * GDN_W, (part + 1) * GDN_W)
        xbuf_ref[0:pad, ps] = xbuf_ref[rows:rows + pad, ps]

    project(0)
    project(1)
    conv(0, q_ref)
    project(2)
    conv(1, k_ref)
    z_ref[...] = jnp.dot(xn, w_ref[:, 3 * GDN_W:], preferred_element_type=F32).astype(BF16)
    ba_ref[...] = jnp.dot(xn, wg_ref[...], preferred_element_type=F32)
    conv(2, v_ref)


def _gdn_block(q_ref, k_ref, v_ref, ba_ref, gp_ref, o_ref, s_ref, qk_ref):
    tb, c, dh, nh = GDN_BLOCK, GDN_CHUNK, GDN_DIM, GDN_HEADS
    nc = tb // c

    ba = ba_ref[...]
    beta = jax.nn.sigmoid(ba)
    xs = ba + gp_ref[1:2, :]
    softplus = jnp.maximum(xs, 0.0) + jnp.log1p(jnp.exp(-jnp.abs(xs)))
    g = -jnp.exp(gp_ref[0:1, :]) * softplus
    shift = c.bit_length() - 1
    ri = lax.broadcasted_iota(jnp.int32, (tb, tb), 0)
    ci = lax.broadcasted_iota(jnp.int32, (tb, tb), 1)
    same = jnp.right_shift(ri, shift) == jnp.right_shift(ci, shift)
    causal = same & (ri >= ci)
    strict = same & (ri > ci)
    tri = jnp.where(causal, 1.0, 0.0).astype(BF16)
    g_hi = g.astype(BF16)
    g_mid = (g - g_hi.astype(F32)).astype(BF16)
    g_lo = (g - g_hi.astype(F32) - g_mid.astype(F32)).astype(BF16)
    gc = sum(jnp.dot(tri, part, preferred_element_type=F32) for part in (g_hi, g_mid, g_lo))
    gct = gc.T
    egc = jnp.exp(gc)
    eye_s = jnp.where(lax.broadcasted_iota(jnp.int32, (c, tb), 0)
                      == jnp.bitwise_and(lax.broadcasted_iota(jnp.int32, (c, tb), 1), c - 1),
                      1.0, 0.0).astype(F32)

    def fold(m):
        out = m[0:c]
        for j in range(1, nc):
            out = out + m[j * c:(j + 1) * c]
        return out

    same_bf = jnp.where(same, 1.0, 0.0).astype(BF16)

    def unfold(m):
        return jnp.concatenate([m.astype(BF16)] * nc, axis=0) * same_bf

    for g0 in range(0, nh, GDN_GROUP):
        heads = list(range(g0, g0 + GDN_GROUP))
        idx = range(len(heads))
        q = [q_ref[:, h * dh:(h + 1) * dh] for h in heads]
        k = [k_ref[:, h * dh:(h + 1) * dh] for h in heads]
        kf = [k[i].astype(F32) for i in idx]
        v = [v_ref[:, h * dh:(h + 1) * dh].astype(F32) for h in heads]
        bcol = [beta[:, h:h + 1] for h in heads]
        gcol = [gc[:, nh + h:nh + h + 1] for h in heads]
        ecol = [egc[:, nh + h:nh + h + 1] for h in heads]
        grow = [gct[nh + h:nh + h + 1, :] for h in heads]
        kb = [kf[i] * bcol[i] for i in idx]
        kq = [_dot_nt(jnp.concatenate([kb[i].astype(BF16), q[i]], axis=0), k[i]) for i in idx]
        decay = [jnp.exp(jnp.minimum(gcol[i] - grow[i], 0.0)) for i in idx]
        low = [jnp.where(strict, kq[i][:tb] * decay[i], 0.0) for i in idx]
        for i in idx:
            qk_ref[heads[i]] = jnp.where(causal, kq[i][tb:] * decay[i], 0.0).astype(BF16)
        ps = [fold(low[i]) for i in idx]
        ts = [eye_s - ps[i] for i in idx]
        ps = [_dot(ps[i], low[i]) for i in idx]
        n_levels = shift - 1
        for lvl in range(n_levels):
            wbd = [unfold(ps[i]) for i in idx]
            if lvl + 1 < n_levels:
                r = [_dot(jnp.concatenate([ts[i], ps[i]], axis=0), wbd[i]) for i in idx]
                ts = [ts[i] + r[i][:c] for i in idx]
                ps = [r[i][c:] for i in idx]
            else:
                ts = [ts[i] + _dot(ts[i], wbd[i]) for i in idx]
        uw = [_dot(unfold(ts[i]), jnp.concatenate([v[i] * bcol[i], kb[i] * ecol[i]], axis=-1))
              for i in idx]
        qd = [q[i].astype(F32) * ecol[i] for i in idx]
        s = [s_ref[h] for h in heads]
        vnew = [[] for _ in idx]
        inter = [[] for _ in idx]
        for j in range(nc):
            r = slice(j * c, (j + 1) * c)
            ws = [_dot(jnp.concatenate([uw[i][r, dh:], qd[i][r]], axis=0), s[i]) for i in idx]
            for i in idx:
                vnew[i].append(uw[i][r, :dh] - ws[i][:c])
                inter[i].append(ws[i][c:])
            gl = [gc[(j + 1) * c - 1:(j + 1) * c, nh + h:nh + h + 1] for h in heads]
            s = [s[i] * jnp.exp(gl[i]) + _dot_tn(kf[i][r] * jnp.exp(gl[i] - gcol[i][r]), vnew[i][j])
                 for i in idx]
        for i in idx:
            h = heads[i]
            s_ref[h] = s[i]
            o = jnp.concatenate(inter[i], axis=0) + _dot(qk_ref[h], jnp.concatenate(vnew[i], axis=0))
            o_ref[:, h * dh:(h + 1) * dh] = o.astype(o_ref.dtype)


def _gdn_mixer_body(x_ref, g_ref, w_ref, wg_ref, cw_ref, gp_ref, o_ref, z_ref,
                    s_ref, qk_ref, xbuf_ref, q_ref, k_ref, v_ref, ba_ref):
    @pl.when(pl.program_id(1) == 0)
    def _():
        s_ref[...] = jnp.zeros_like(s_ref)

    _gdn_proj_body(x_ref, g_ref, w_ref, wg_ref, cw_ref, q_ref, k_ref, v_ref, z_ref, ba_ref, xbuf_ref)
    for sb in range(GDN_STEP // GDN_BLOCK):
        rows = pl.ds(sb * GDN_BLOCK, GDN_BLOCK)
        _gdn_block(q_ref.at[rows], k_ref.at[rows], v_ref.at[rows], ba_ref.at[rows], gp_ref,
                   o_ref.at[rows], s_ref, qk_ref.at[sb])


def _gdn_mixer(x, gain, w, w_gate, conv_w, layer, a_log, dt_bias, batch, seq):
    n, d = x.shape
    nt = seq // PROJ_ROWS
    lane_pad = GATE_LANES - 2 * GDN_HEADS
    gate_params = jnp.stack([
        jnp.pad(a_log.astype(F32), (GDN_HEADS, lane_pad)),
        jnp.pad(dt_bias.astype(F32), (GDN_HEADS, lane_pad))])
    gate_params = jnp.pad(gate_params, ((0, SUBLANES - 2), (0, 0)))
    blk = lambda width: pl.BlockSpec((PROJ_ROWS, width), lambda b, t: (b * nt + t, 0))
    act = pltpu.VMEM((PROJ_ROWS, GDN_W), BF16)
    return pl.pallas_call(
        _gdn_mixer_body,
        grid=(batch, nt),
        in_specs=[blk(d), _resident((1, d)), _layer(w, layer, 4 * GDN_W), _layer(w_gate, layer),
                  _layer(conv_w, layer), _resident((SUBLANES, GATE_LANES))],
        out_specs=[blk(GDN_W), blk(GDN_W)],
        out_shape=[jax.ShapeDtypeStruct((n, GDN_W), BF16)] * 2,
        scratch_shapes=[pltpu.VMEM((GDN_HEADS, GDN_DIM, GDN_DIM), F32),
                        pltpu.VMEM((GDN_STEP // GDN_BLOCK, GDN_HEADS, GDN_BLOCK, GDN_BLOCK), BF16),
                        pltpu.VMEM((SUBLANES + PROJ_ROWS, 3 * GDN_W), F32),
                        act, act, act, pltpu.VMEM((PROJ_ROWS, GATE_LANES), F32)],
        compiler_params=pltpu.CompilerParams(dimension_semantics=("parallel", "arbitrary"),
                                             vmem_limit_bytes=VMEM_LIMIT),
        name="gdn_mixer",
    )(x, gain.reshape(1, d), w, w_gate, conv_w, gate_params)


def _out_mlp_body(head_dim, center, o_ref, gate_ref, hg_ref, h_ref, wo_ref, gains_ref, wu_ref,
                  wd_ref, out_ref, y_ref):
    subs = [slice(r * MLP_SUB, (r + 1) * MLP_SUB) for r in range(MLP_ROWS // MLP_SUB)]
    for rs in subs:
        for hd in range(o_ref.shape[1] // head_dim):
            hs = slice(hd * head_dim, (hd + 1) * head_dim)
            o = o_ref[rs, hs].astype(F32)
            if center:
                o = o - jnp.mean(o, axis=-1, keepdims=True)
            on = o * lax.rsqrt(jnp.mean(o * o, axis=-1, keepdims=True) + EPS) * hg_ref[:, hs]
            y_ref[rs, hs] = (on * _silu(gate_ref[rs, hs].astype(F32))).astype(BF16)
    mix = [jnp.dot(y_ref[rs, :], wo_ref[...], preferred_element_type=F32) for rs in subs]
    h1 = [h_ref[rs, :] + _rms(m, gains_ref[1:2, :]) for rs, m in zip(subs, mix)]
    xn = [_rms(a, gains_ref[2:3, :]).astype(BF16) for a in h1]
    ff = [jnp.zeros_like(a) for a in h1]
    for c in range(D_FF // COL_CHUNK):
        sl = slice(c * COL_CHUNK, (c + 1) * COL_CHUNK)
        up = [jnp.dot(a, wu_ref[:, sl], preferred_element_type=F32) for a in xn]
        act = [jnp.square(jnp.maximum(u, 0.0)).astype(BF16) for u in up]
        ff = [f + jnp.dot(a, wd_ref[sl, :], preferred_element_type=F32) for f, a in zip(ff, act)]
    for rs, a, f in zip(subs, h1, ff):
        out_ref[rs, :] = a + _rms(f, gains_ref[3:4, :])


def _out_mlp(o, gate, head_gain, head_dim, center, h, w_out, mixer_layer, gains, w_up, w_down, layer):
    n, d = h.shape
    k_in = o.shape[1]
    return pl.pallas_call(
        functools.partial(_out_mlp_body, head_dim, center),
        grid=(n // MLP_ROWS,),
        in_specs=[pl.BlockSpec((MLP_ROWS, k_in), lambda i: (i, 0)),
                  pl.BlockSpec((MLP_ROWS, k_in), lambda i: (i, 0)),
                  _resident((1, k_in)),
                  pl.BlockSpec((MLP_ROWS, d), lambda i: (i, 0)),
                  _layer(w_out, mixer_layer), _layer(gains, layer),
                  _layer(w_up, layer), _layer(w_down, layer)],
        out_specs=pl.BlockSpec((MLP_ROWS, d), lambda i: (i, 0)),
        out_shape=jax.ShapeDtypeStruct((n, d), F32),
        scratch_shapes=[pltpu.VMEM((MLP_ROWS, k_in), BF16)],
        compiler_params=pltpu.CompilerParams(dimension_semantics=("parallel",),
                                             vmem_limit_bytes=VMEM_LIMIT),
        name="out_mlp",
    )(o, gate, head_gain, h, w_out, gains, w_up, w_down)


def kernel(x, norm_gains, ret_w_in, ret_gn_gain, ret_w_out, gdn_w_in, gdn_conv_w, gdn_a_log,
           gdn_dt_bias, gdn_norm_gain, gdn_w_out, mlp_w_up, mlp_w_down):
    batch, seq, d = x.shape
    h = x.reshape(batch * seq, d)
    rope_base, rope_off, intra, q_dec, k_dec, chunk_dec = _retention_tables(seq)
    gains = norm_gains.astype(F32)
    n_main = 4 * GDN_W
    ret_w_in, ret_w_out, gdn_w_out, mlp_w_up, mlp_w_down = (
        w.astype(BF16) for w in (ret_w_in, ret_w_out, gdn_w_out, mlp_w_up, mlp_w_down))
    gdn_w_main = gdn_w_in.astype(BF16)
    gdn_w_gate = jnp.pad(gdn_w_in[:, :, n_main:],
                         ((0, 0), (0, 0), (0, GATE_LANES - 2 * GDN_HEADS))).astype(BF16)
    gdn_conv_w = gdn_conv_w.astype(F32)
    for i in range(DEPTH):
        j = i // 2
        if i % 2 == 0:
            o, gate = _retention_mixer(h, gains[i, 0], ret_w_in, j, rope_base, rope_off, q_dec, k_dec,
                                       intra, chunk_dec, batch, seq)
            head_gain = ret_gn_gain[j].astype(F32).reshape(1, RET_V_W)
            h = _out_mlp(o, gate, head_gain, RET_V_DIM, True, h, ret_w_out, j, gains,
                         mlp_w_up, mlp_w_down, i)
        else:
            o, z = _gdn_mixer(h, gains[i, 0], gdn_w_main, gdn_w_gate, gdn_conv_w, j,
                              gdn_a_log[j], gdn_dt_bias[j], batch, seq)
            head_gain = jnp.tile(gdn_norm_gain[j].astype(F32), GDN_HEADS).reshape(1, GDN_W)
            h = _out_mlp(o, z, head_gain, GDN_DIM, False, h, gdn_w_out, j, gains,
                         mlp_w_up, mlp_w_down, i)
    return h.reshape(batch, seq, d)
```

```python
import functools

import jax
import jax.numpy as jnp
from jax import lax
from jax.experimental import pallas as pl
from jax.experimental.pallas import tpu as pltpu

F32 = jnp.float32
BF16 = jnp.bfloat16

D_MODEL = 1024
DEPTH = 4
EPS = 1e-6
LANES = 128
SUBLANES = 8

RET_HEADS = 4
RET_QK_DIM = D_MODEL // RET_HEADS
RET_V_DIM = 2 * D_MODEL // RET_HEADS
RET_QK_W = RET_HEADS * RET_QK_DIM
RET_V_W = RET_HEADS * RET_V_DIM
RET_CHUNK = 256
RET_STEP = 512
ROPE_BASE = 10000.0

GDN_HEADS = 8
GDN_DIM = D_MODEL // GDN_HEADS
GDN_W = GDN_HEADS * GDN_DIM
GDN_CHUNK = 64
GDN_BLOCK = 256
GDN_STEP = 512
GDN_GROUP = 8
CONV_WIDTH = 4
CONV_COLS = 512
GATE_LANES = LANES
D_FF = 4 * D_MODEL

PROJ_ROWS = 512
PROJ_SUB = 256
MLP_ROWS = 512
MLP_SUB = 256
COL_CHUNK = 1024
VMEM_LIMIT = 56 * 1024 * 1024


def _rms(x, gain):
    return x * lax.rsqrt(jnp.mean(x * x, axis=-1, keepdims=True) + EPS) * gain


def _dot(a, b):
    return jnp.dot(a.astype(BF16), b.astype(BF16), preferred_element_type=F32)


def _dot_nt(a, b):
    return lax.dot_general(a.astype(BF16), b.astype(BF16), (((1,), (1,)), ((), ())),
                           preferred_element_type=F32)


def _dot_tn(a, b):
    return lax.dot_general(a.astype(BF16), b.astype(BF16), (((0,), (0,)), ((), ())),
                           preferred_element_type=F32)


def _silu(x):
    half = 0.5 * x
    return half * jnp.tanh(half) + half


def _l2norm(x):
    return x * lax.rsqrt(jnp.sum(x * x, axis=-1, keepdims=True) + EPS)


def _resident(shape):
    zeros = (0,) * len(shape)
    return pl.BlockSpec(shape, lambda *_: zeros, pipeline_mode=pl.Buffered(1))


def _layer(stacked, layer, cols=None):
    shape = stacked.shape[1:] if cols is None else stacked.shape[1:-1] + (cols,)
    zeros = (0,) * len(shape)
    return pl.BlockSpec((None,) + shape, lambda *_: (layer,) + zeros, pipeline_mode=pl.Buffered(1))


def _ret_proj_body(x_ref, g_ref, w_ref, base_ref, off_ref, qdec_ref, kdec_ref,
                   q_ref, k_ref, qd_ref, kd_ref, v_ref, gate_ref):
    half = RET_QK_DIM // 2
    subs = [slice(r * PROJ_SUB, (r + 1) * PROJ_SUB) for r in range(PROJ_ROWS // PROJ_SUB)]
    xn = [_rms(x_ref[rs, :], g_ref[...]).astype(BF16) for rs in subs]

    def project(c):
        return [jnp.dot(a, w_ref[:, c * COL_CHUNK:(c + 1) * COL_CHUNK], preferred_element_type=F32)
                for a in xn]

    def rotary(accs, scale, dec_ref, out_ref, outd_ref):
        for rs, acc in zip(subs, accs):
            cos = base_ref[0:1, :] * off_ref[0, rs, :] - base_ref[1:2, :] * off_ref[1, rs, :]
            sin = base_ref[1:2, :] * off_ref[0, rs, :] + base_ref[0:1, :] * off_ref[1, rs, :]
            for h in range(RET_HEADS):
                base = h * RET_QK_DIM
                x1 = acc[:, base:base + half]
                x2 = acc[:, base + half:base + 2 * half]
                r1 = (x1 * cos - x2 * sin) * scale
                r2 = (x1 * sin + x2 * cos) * scale
                dec = dec_ref[h, rs, :]
                out_ref[rs, base:base + half] = r1.astype(BF16)
                out_ref[rs, base + half:base + 2 * half] = r2.astype(BF16)
                outd_ref[rs, base:base + half] = (r1 * dec).astype(BF16)
                outd_ref[rs, base + half:base + 2 * half] = (r2 * dec).astype(BF16)

    rotary(project(0), 1.0, qdec_ref, q_ref, qd_ref)
    rotary(project(1), RET_QK_DIM ** -0.5, kdec_ref, k_ref, kd_ref)
    n_qk = 2 * RET_QK_W // COL_CHUNK
    n_v = RET_V_W // COL_CHUNK
    for c in range(n_v):
        cs = slice(c * COL_CHUNK, (c + 1) * COL_CHUNK)
        for rs, acc in zip(subs, project(n_qk + c)):
            v_ref[rs, cs] = acc.astype(BF16)
        for rs, acc in zip(subs, project(n_qk + n_v + c)):
            gate_ref[rs, cs] = acc.astype(BF16)


def _retention_tables(seq):
    half = RET_QK_DIM // 2
    inv_freq = ROPE_BASE ** (-jnp.arange(half, dtype=F32) / half)
    base = jnp.arange(0, seq, PROJ_ROWS).astype(F32)[:, None] * inv_freq[None, :]
    off = jnp.arange(PROJ_ROWS).astype(F32)[:, None] * inv_freq[None, :]
    log_gamma = jnp.log1p(-(2.0 ** (-5.0 - jnp.arange(RET_HEADS, dtype=F32))))
    idx = jnp.arange(RET_CHUNK, dtype=F32)
    rel = idx[:, None] - idx[None, :]
    causal = rel >= 0
    intra = jnp.where(causal, jnp.exp(log_gamma[:, None, None] * jnp.where(causal, rel, 0.0)), 0.0)
    q_dec = jnp.exp(log_gamma[:, None] * (idx + 1.0))[..., None]
    k_dec = jnp.exp(log_gamma[:, None] * (RET_CHUNK - 1.0 - idx))[..., None]
    chunk_dec = jnp.exp(log_gamma * RET_CHUNK)[:, None, None]
    reps = PROJ_ROWS // RET_CHUNK
    tile = lambda d: jnp.tile(jnp.broadcast_to(d, (RET_HEADS, RET_CHUNK, half)), (1, reps, 1))
    return (jnp.stack([jnp.cos(base), jnp.sin(base)], axis=1), jnp.stack([jnp.cos(off), jnp.sin(off)]),
            intra, tile(q_dec), tile(k_dec),
            jnp.broadcast_to(chunk_dec, (RET_HEADS, 1, RET_V_DIM)))


def _ret_body(q_ref, k_ref, qd_ref, kd_ref, v_ref, intra_ref, cdec_ref, o_ref, s_ref):
    @pl.when(pl.program_id(1) == 0)
    def _():
        s_ref[...] = jnp.zeros_like(s_ref)

    for j in range(RET_STEP // RET_CHUNK):
        rows = slice(j * RET_CHUNK, (j + 1) * RET_CHUNK)
        for h in range(RET_HEADS):
            qs = slice(h * RET_QK_DIM, (h + 1) * RET_QK_DIM)
            vs = slice(h * RET_V_DIM, (h + 1) * RET_V_DIM)
            v = v_ref[rows, vs]
            s = s_ref[h]
            scores = _dot_nt(q_ref[rows, qs], k_ref[rows, qs]) * intra_ref[h]
            o = _dot(scores, v) + _dot(qd_ref[rows, qs], s)
            s_ref[h] = cdec_ref[h] * s + _dot_tn(kd_ref[rows, qs], v)
            o_ref[rows, vs] = o.astype(o_ref.dtype)


def _ret_mixer_body(x_ref, g_ref, w_ref, base_ref, off_ref, qdec_ref, kdec_ref, intra_ref, cdec_ref,
                    o_ref, gate_ref, s_ref, q_ref, k_ref, qd_ref, kd_ref, v_ref):
    _ret_proj_body(x_ref, g_ref, w_ref, base_ref, off_ref, qdec_ref, kdec_ref,
                   q_ref, k_ref, qd_ref, kd_ref, v_ref, gate_ref)
    _ret_body(q_ref, k_ref, qd_ref, kd_ref, v_ref, intra_ref, cdec_ref, o_ref, s_ref)


def _retention_mixer(x, gain, w, layer, rope_base, rope_off, q_dec, k_dec, intra, chunk_dec, batch, seq):
    n, d = x.shape
    half = RET_QK_DIM // 2
    nt = seq // PROJ_ROWS
    blk = lambda width: pl.BlockSpec((PROJ_ROWS, width), lambda b, t: (b * nt + t, 0))
    base = pl.BlockSpec((None, 2, half), lambda b, t: (t, 0, 0))
    qk_scratch = pltpu.VMEM((PROJ_ROWS, RET_QK_W), BF16)
    return pl.pallas_call(
        _ret_mixer_body,
        grid=(batch, nt),
        in_specs=[blk(d), _resident((1, d)), _layer(w, layer), base, _resident(rope_off.shape),
                  _resident(q_dec.shape), _resident(k_dec.shape),
                  _resident(intra.shape), _resident(chunk_dec.shape)],
        out_specs=[blk(RET_V_W), blk(RET_V_W)],
        out_shape=[jax.ShapeDtypeStruct((n, RET_V_W), BF16)] * 2,
        scratch_shapes=[pltpu.VMEM((RET_HEADS, RET_QK_DIM, RET_V_DIM), F32),
                        qk_scratch, qk_scratch, qk_scratch, qk_scratch,
                        pltpu.VMEM((PROJ_ROWS, RET_V_W), BF16)],
        compiler_params=pltpu.CompilerParams(dimension_semantics=("parallel", "arbitrary"),
                                             vmem_limit_bytes=VMEM_LIMIT),
        name="retention_mixer",
    )(x, gain.reshape(1, d), w, rope_base, rope_off, q_dec, k_dec, intra, chunk_dec)


def _gdn_proj_body(x_ref, g_ref, w_ref, wg_ref, cw_ref,
                   q_ref, k_ref, v_ref, z_ref, ba_ref, xbuf_ref):
    rows, pad, dh = PROJ_ROWS, SUBLANES, GDN_DIM

    @pl.when(pl.program_id(1) == 0)
    def _():
        xbuf_ref[0:pad, :] = jnp.zeros((pad, 3 * GDN_W), F32)

    xn = _rms(x_ref[...], g_ref[...]).astype(BF16)

    def project(part):
        ps = slice(part * GDN_W, (part + 1) * GDN_W)
        xbuf_ref[pad:, ps] = jnp.dot(xn, w_ref[:, ps], preferred_element_type=F32)

    def conv(part, out_ref):
        for cb in range(GDN_W // CONV_COLS):
            lo = part * GDN_W + cb * CONV_COLS
            cs = slice(lo, lo + CONV_COLS)
            full = xbuf_ref[:, cs]
            prev = pltpu.roll(full, 1, axis=0)
            newer = cw_ref[3:4, cs] * full + cw_ref[2:3, cs] * prev
            older = cw_ref[1:2, cs] * full + cw_ref[0:1, cs] * prev
            acc = (newer + pltpu.roll(older, 2, axis=0))[pad:]
            act = _silu(acc)
            for hh in range(CONV_COLS // dh):
                a = act[:, hh * dh:(hh + 1) * dh]
                if part == 0:
                    a = a * (lax.rsqrt(jnp.sum(a * a, axis=-1, keepdims=True) + EPS) * (dh ** -0.5))
                elif part == 1:
                    a = _l2norm(a)
                oc = cb * CONV_COLS + hh * dh
                out_ref[:, oc:oc + dh] = a.astype(BF16)
        ps = slice(part * GDN_W, (part + 1) * GDN_W)
        xbuf_ref[0:pad, ps] = xbuf_ref[rows:rows + pad, ps]

    project(0)
    project(1)
    conv(0, q_ref)
    project(2)
    conv(1, k_ref)
    z_ref[...] = jnp.dot(xn, w_ref[:, 3 * GDN_W:], preferred_element_type=F32).astype(BF16)
    ba_ref[...] = jnp.dot(xn, wg_ref[...], preferred_element_type=F32)
    conv(2, v_ref)


def _gdn_block(q_ref, k_ref, v_ref, ba_ref, gp_ref, o_ref, s_ref, qk_ref):
    tb, c, dh, nh = GDN_BLOCK, GDN_CHUNK, GDN_DIM, GDN_HEADS
    nc = tb // c

    ba = ba_ref[...]
    beta = jax.nn.sigmoid(ba)
    xs = ba + gp_ref[1:2, :]
    softplus = jnp.maximum(xs, 0.0) + jnp.log1p(jnp.exp(-jnp.abs(xs)))
    g = -jnp.exp(gp_ref[0:1, :]) * softplus
    shift = c.bit_length() - 1
    ri = lax.broadcasted_iota(jnp.int32, (tb, tb), 0)
    ci = lax.broadcasted_iota(jnp.int32, (tb, tb), 1)
    same = jnp.right_shift(ri, shift) == jnp.right_shift(ci, shift)
    causal = same & (ri >= ci)
    strict = same & (ri > ci)
    tri = jnp.where(causal, 1.0, 0.0).astype(BF16)
    g_hi = g.astype(BF16)
    g_mid = (g - g_hi.astype(F32)).astype(BF16)
    g_lo = (g - g_hi.astype(F32) - g_mid.astype(F32)).astype(BF16)
    gc = sum(jnp.dot(tri, part, preferred_element_type=F32) for part in (g_hi, g_mid, g_lo))
    gct = gc.T
    egc = jnp.exp(gc)
    eye_s = jnp.where(lax.broadcasted_iota(jnp.int32, (c, tb), 0)
                      == jnp.bitwise_and(lax.broadcasted_iota(jnp.int32, (c, tb), 1), c - 1),
                      1.0, 0.0).astype(F32)

    def fold(m):
        out = m[0:c]
        for j in range(1, nc):
            out = out + m[j * c:(j + 1) * c]
        return out

    same_bf = jnp.where(same, 1.0, 0.0).astype(BF16)
    row_s = lax.broadcasted_iota(jnp.int32, (c, tb), 0)
    col_s = jnp.bitwise_and(lax.broadcasted_iota(jnp.int32, (c, tb), 1), c - 1)

    def pair_lower(lvl):
        rb = jnp.right_shift(row_s, lvl)
        cb = jnp.right_shift(col_s, lvl)
        return (jnp.right_shift(rb, 1) == jnp.right_shift(cb, 1)) & (jnp.bitwise_and(rb, 1) == 1) \
            & (jnp.bitwise_and(cb, 1) == 0)

    def unfold(m):
        return jnp.concatenate([m.astype(BF16)] * nc, axis=0) * same_bf

    for g0 in range(0, nh, GDN_GROUP):
        heads = list(range(g0, g0 + GDN_GROUP))
        idx = range(len(heads))
        q = [q_ref[:, h * dh:(h + 1) * dh] for h in heads]
        k = [k_ref[:, h * dh:(h + 1) * dh] for h in heads]
        kf = [k[i].astype(F32) for i in idx]
        v = [v_ref[:, h * dh:(h + 1) * dh].astype(F32) for h in heads]
        bcol = [beta[:, h:h + 1] for h in heads]
        gcol = [gc[:, nh + h:nh + h + 1] for h in heads]
        ecol = [egc[:, nh + h:nh + h + 1] for h in heads]
        grow = [gct[nh + h:nh + h + 1, :] for h in heads]
        kb = [kf[i] * bcol[i] for i in idx]
        kq = [_dot_nt(jnp.concatenate([kb[i].astype(BF16), q[i]], axis=0), k[i]) for i in idx]
        decay = [jnp.exp(jnp.minimum(gcol[i] - grow[i], 0.0)) for i in idx]
        low = [jnp.where(strict, kq[i][:tb] * decay[i], 0.0) for i in idx]
        for i in idx:
            qk_ref[heads[i]] = jnp.where(causal, kq[i][tb:] * decay[i], 0.0).astype(BF16)
        low_s = [fold(low[i]) for i in idx]
        ts = [eye_s - jnp.where(pair_lower(0), low_s[i], 0.0) for i in idx]
        for lvl in range(1, shift):
            off = pair_lower(lvl)
            cx = [_dot(jnp.where(off, low_s[i], 0.0), unfold(ts[i])) for i in idx]
            ts = [ts[i] - _dot(ts[i], unfold(cx[i])) for i in idx]
        uw = [_dot(unfold(ts[i]), jnp.concatenate([v[i] * bcol[i], kb[i] * ecol[i]], axis=-1))
              for i in idx]
        qd = [q[i].astype(F32) * ecol[i] for i in idx]
        s = [s_ref[h] for h in heads]
        vnew = [[] for _ in idx]
        inter = [[] for _ in idx]
        for j in range(nc):
            r = slice(j * c, (j + 1) * c)
            ws = [_dot(jnp.concatenate([uw[i][r, dh:], qd[i][r]], axis=0), s[i]) for i in idx]
            for i in idx:
                vnew[i].append(uw[i][r, :dh] - ws[i][:c])
                inter[i].append(ws[i][c:])
            gl = [gc[(j + 1) * c - 1:(j + 1) * c, nh + h:nh + h + 1] for h in heads]
            s = [s[i] * jnp.exp(gl[i]) + _dot_tn(kf[i][r] * jnp.exp(gl[i] - gcol[i][r]), vnew[i][j])
                 for i in idx]
        for i in idx:
            h = heads[i]
            s_ref[h] = s[i]
            o = jnp.concatenate(inter[i], axis=0) + _dot(qk_ref[h], jnp.concatenate(vnew[i], axis=0))
            o_ref[:, h * dh:(h + 1) * dh] = o.astype(o_ref.dtype)


def _gdn_mixer_body(x_ref, g_ref, w_ref, wg_ref, cw_ref, gp_ref, o_ref, z_ref,
                    s_ref, qk_ref, xbuf_ref, q_ref, k_ref, v_ref, ba_ref):
    @pl.when(pl.program_id(1) == 0)
    def _():
        s_ref[...] = jnp.zeros_like(s_ref)

    _gdn_proj_body(x_ref, g_ref, w_ref, wg_ref, cw_ref, q_ref, k_ref, v_ref, z_ref, ba_ref, xbuf_ref)
    for sb in range(GDN_STEP // GDN_BLOCK):
        rows = pl.ds(sb * GDN_BLOCK, GDN_BLOCK)
        _gdn_block(q_ref.at[rows], k_ref.at[rows], v_ref.at[rows], ba_ref.at[rows], gp_ref,
                   o_ref.at[rows], s_ref, qk_ref.at[sb])


def _gdn_mixer(x, gain, w, w_gate, conv_w, layer, a_log, dt_bias, batch, seq):
    n, d = x.shape
    nt = seq // PROJ_ROWS
    lane_pad = GATE_LANES - 2 * GDN_HEADS
    gate_params = jnp.stack([
        jnp.pad(a_log.astype(F32), (GDN_HEADS, lane_pad)),
        jnp.pad(dt_bias.astype(F32), (GDN_HEADS, lane_pad))])
    gate_params = jnp.pad(gate_params, ((0, SUBLANES - 2), (0, 0)))
    blk = lambda width: pl.BlockSpec((PROJ_ROWS, width), lambda b, t: (b * nt + t, 0))
    act = pltpu.VMEM((PROJ_ROWS, GDN_W), BF16)
    return pl.pallas_call(
        _gdn_mixer_body,
        grid=(batch, nt),
        in_specs=[blk(d), _resident((1, d)), _layer(w, layer, 4 * GDN_W), _layer(w_gate, layer),
                  _layer(conv_w, layer), _resident((SUBLANES, GATE_LANES))],
        out_specs=[blk(GDN_W), blk(GDN_W)],
        out_shape=[jax.ShapeDtypeStruct((n, GDN_W), BF16)] * 2,
        scratch_shapes=[pltpu.VMEM((GDN_HEADS, GDN_DIM, GDN_DIM), F32),
                        pltpu.VMEM((GDN_STEP // GDN_BLOCK, GDN_HEADS, GDN_BLOCK, GDN_BLOCK), BF16),
                        pltpu.VMEM((SUBLANES + PROJ_ROWS, 3 * GDN_W), F32),
                        act, act, act, pltpu.VMEM((PROJ_ROWS, GATE_LANES), F32)],
        compiler_params=pltpu.CompilerParams(dimension_semantics=("parallel", "arbitrary"),
                                             vmem_limit_bytes=VMEM_LIMIT),
        name="gdn_mixer",
    )(x, gain.reshape(1, d), w, w_gate, conv_w, gate_params)


def _out_mlp_body(head_dim, center, o_ref, gate_ref, hg_ref, h_ref, wo_ref, gains_ref, wu_ref,
                  wd_ref, out_ref, y_ref):
    subs = [slice(r * MLP_SUB, (r + 1) * MLP_SUB) for r in range(MLP_ROWS // MLP_SUB)]
    for rs in subs:
        for hd in range(o_ref.shape[1] // head_dim):
            hs = slice(hd * head_dim, (hd + 1) * head_dim)
            o = o_ref[rs, hs].astype(F32)
            if center:
                o = o - jnp.mean(o, axis=-1, keepdims=True)
            on = o * lax.rsqrt(jnp.mean(o * o, axis=-1, keepdims=True) + EPS) * hg_ref[:, hs]
            y_ref[rs, hs] = (on * _silu(gate_ref[rs, hs].astype(F32))).astype(BF16)
    mix = [jnp.dot(y_ref[rs, :], wo_ref[...], preferred_element_type=F32) for rs in subs]
    h1 = [h_ref[rs, :] + _rms(m, gains_ref[1:2, :]) for rs, m in zip(subs, mix)]
    xn = [_rms(a, gains_ref[2:3, :]).astype(BF16) for a in h1]
    ff = [jnp.zeros_like(a) for a in h1]
    for c in range(D_FF // COL_CHUNK):
        sl = slice(c * COL_CHUNK, (c + 1) * COL_CHUNK)
        up = [jnp.dot(a, wu_ref[:, sl], preferred_element_type=F32) for a in xn]
        act = [jnp.square(jnp.maximum(u, 0.0)).astype(BF16) for u in up]
        ff = [f + jnp.dot(a, wd_ref[sl, :], preferred_element_type=F32) for f, a in zip(ff, act)]
    for rs, a, f in zip(subs, h1, ff):
        out_ref[rs, :] = a + _rms(f, gains_ref[3:4, :])


def _out_mlp(o, gate, head_gain, head_dim, center, h, w_out, mixer_layer, gains, w_up, w_down, layer):
    n, d = h.shape
    k_in = o.shape[1]
    return pl.pallas_call(
        functools.partial(_out_mlp_body, head_dim, center),
        grid=(n // MLP_ROWS,),
        in_specs=[pl.BlockSpec((MLP_ROWS, k_in), lambda i: (i, 0)),
                  pl.BlockSpec((MLP_ROWS, k_in), lambda i: (i, 0)),
                  _resident((1, k_in)),
                  pl.BlockSpec((MLP_ROWS, d), lambda i: (i, 0)),
                  _layer(w_out, mixer_layer), _layer(gains, layer),
                  _layer(w_up, layer), _layer(w_down, layer)],
        out_specs=pl.BlockSpec((MLP_ROWS, d), lambda i: (i, 0)),
        out_shape=jax.ShapeDtypeStruct((n, d), F32),
        scratch_shapes=[pltpu.VMEM((MLP_ROWS, k_in), BF16)],
        compiler_params=pltpu.CompilerParams(dimension_semantics=("parallel",),
                                             vmem_limit_bytes=VMEM_LIMIT),
        name="out_mlp",
    )(o, gate, head_gain, h, w_out, gains, w_up, w_down)


def kernel(x, norm_gains, ret_w_in, ret_gn_gain, ret_w_out, gdn_w_in, gdn_conv_w, gdn_a_log,
           gdn_dt_bias, gdn_norm_gain, gdn_w_out, mlp_w_up, mlp_w_down):
    batch, seq, d = x.shape
    h = x.reshape(batch * seq, d)
    rope_base, rope_off, intra, q_dec, k_dec, chunk_dec = _retention_tables(seq)
    gains = norm_gains.astype(F32)
    n_main = 4 * GDN_W
    ret_w_in, ret_w_out, gdn_w_out, mlp_w_up, mlp_w_down = (
        w.astype(BF16) for w in (ret_w_in, ret_w_out, gdn_w_out, mlp_w_up, mlp_w_down))
    gdn_w_main = gdn_w_in.astype(BF16)
    gdn_w_gate = jnp.pad(gdn_w_in[:, :, n_main:],
                         ((0, 0), (0, 0), (0, GATE_LANES - 2 * GDN_HEADS))).astype(BF16)
    gdn_conv_w = gdn_conv_w.astype(F32)
    for i in range(DEPTH):
        j = i // 2
        if i % 2 == 0:
            o, gate = _retention_mixer(h, gains[i, 0], ret_w_in, j, rope_base, rope_off, q_dec, k_dec,
                                       intra, chunk_dec, batch, seq)
            head_gain = ret_gn_gain[j].astype(F32).reshape(1, RET_V_W)
            h = _out_mlp(o, gate, head_gain, RET_V_DIM, True, h, ret_w_out, j, gains,
                         mlp_w_up, mlp_w_down, i)
        else:
            o, z = _gdn_mixer(h, gains[i, 0], gdn_w_main, gdn_w_gate, gdn_conv_w, j,
                              gdn_a_log[j], gdn_dt_bias[j], batch, seq)
            head_gain = jnp.tile(gdn_norm_gain[j].astype(F32), GDN_HEADS).reshape(1, GDN_W)
            h = _out_mlp(o, z, head_gain, GDN_DIM, False, h, gdn_w_out, j, gains,
                         mlp_w_up, mlp_w_down, i)
    return h.reshape(batch, seq, d)
```

```python
import functools

import jax
import jax.numpy as jnp
from jax import lax
from jax.experimental import pallas as pl
from jax.experimental.pallas import tpu as pltpu

F32 = jnp.float32
BF16 = jnp.bfloat16

D_MODEL = 1024
DEPTH = 4
EPS = 1e-6
LANES = 128
SUBLANES = 8

RET_HEADS = 4
RET_QK_DIM = D_MODEL // RET_HEADS
RET_V_DIM = 2 * D_MODEL // RET_HEADS
RET_QK_W = RET_HEADS * RET_QK_DIM
RET_V_W = RET_HEADS * RET_V_DIM
RET_CHUNK = 256
RET_STEP = 512
ROPE_BASE = 10000.0

GDN_HEADS = 8
GDN_DIM = D_MODEL // GDN_HEADS
GDN_W = GDN_HEADS * GDN_DIM
GDN_CHUNK = 64
GDN_BLOCK = 256
GDN_STEP = 256
GDN_GROUP = 8
CONV_WIDTH = 4
CONV_COLS = 512
GATE_LANES = LANES
D_FF = 4 * D_MODEL

PROJ_ROWS = 512
PROJ_SUB = 256
MLP_ROWS = 512
MLP_SUB = 256
COL_CHUNK = 1024
VMEM_LIMIT = 56 * 1024 * 1024


def _rms(x, gain):
    return x * lax.rsqrt(jnp.mean(x * x, axis=-1, keepdims=True) + EPS) * gain


def _dot(a, b):
    return jnp.dot(a.astype(BF16), b.astype(BF16), preferred_element_type=F32)


def _dot_nt(a, b):
    return lax.dot_general(a.astype(BF16), b.astype(BF16), (((1,), (1,)), ((), ())),
                           preferred_element_type=F32)


def _dot_tn(a, b):
    return lax.dot_general(a.astype(BF16), b.astype(BF16), (((0,), (0,)), ((), ())),
                           preferred_element_type=F32)


def _silu(x):
    half = 0.5 * x
    return half * jnp.tanh(half) + half


def _l2norm(x):
    return x * lax.rsqrt(jnp.sum(x * x, axis=-1, keepdims=True) + EPS)


def _resident(shape):
    zeros = (0,) * len(shape)
    return pl.BlockSpec(shape, lambda *_: zeros, pipeline_mode=pl.Buffered(1))


def _layer(stacked, layer, cols=None):
    shape = stacked.shape[1:] if cols is None else stacked.shape[1:-1] + (cols,)
    zeros = (0,) * len(shape)
    return pl.BlockSpec((None,) + shape, lambda *_: (layer,) + zeros, pipeline_mode=pl.Buffered(1))


def _ret_proj_body(x_ref, g_ref, w_ref, base_ref, off_ref, qdec_ref, kdec_ref,
                   q_ref, k_ref, qd_ref, kd_ref, v_ref, gate_ref):
    half = RET_QK_DIM // 2
    subs = [slice(r * PROJ_SUB, (r + 1) * PROJ_SUB) for r in range(PROJ_ROWS // PROJ_SUB)]
    xn = [_rms(x_ref[rs, :], g_ref[...]).astype(BF16) for rs in subs]

    def project(c):
        return [jnp.dot(a, w_ref[:, c * COL_CHUNK:(c + 1) * COL_CHUNK], preferred_element_type=F32)
                for a in xn]

    def rotary(accs, scale, dec_ref, out_ref, outd_ref):
        for rs, acc in zip(subs, accs):
            cos = base_ref[0:1, :] * off_ref[0, rs, :] - base_ref[1:2, :] * off_ref[1, rs, :]
            sin = base_ref[1:2, :] * off_ref[0, rs, :] + base_ref[0:1, :] * off_ref[1, rs, :]
            for h in range(RET_HEADS):
                base = h * RET_QK_DIM
                x1 = acc[:, base:base + half]
                x2 = acc[:, base + half:base + 2 * half]
                r1 = (x1 * cos - x2 * sin) * scale
                r2 = (x1 * sin + x2 * cos) * scale
                dec = dec_ref[h, rs, :]
                out_ref[rs, base:base + half] = r1.astype(BF16)
                out_ref[rs, base + half:base + 2 * half] = r2.astype(BF16)
                outd_ref[rs, base:base + half] = (r1 * dec).astype(BF16)
                outd_ref[rs, base + half:base + 2 * half] = (r2 * dec).astype(BF16)

    rotary(project(0), 1.0, qdec_ref, q_ref, qd_ref)
    rotary(project(1), RET_QK_DIM ** -0.5, kdec_ref, k_ref, kd_ref)
    n_qk = 2 * RET_QK_W // COL_CHUNK
    n_v = RET_V_W // COL_CHUNK
    for c in range(n_v):
        cs = slice(c * COL_CHUNK, (c + 1) * COL_CHUNK)
        for rs, acc in zip(subs, project(n_qk + c)):
            v_ref[rs, cs] = acc.astype(BF16)
        for rs, acc in zip(subs, project(n_qk + n_v + c)):
            gate_ref[rs, cs] = acc.astype(BF16)


def _retention_tables(seq):
    half = RET_QK_DIM // 2
    inv_freq = ROPE_BASE ** (-jnp.arange(half, dtype=F32) / half)
    base = jnp.arange(0, seq, PROJ_ROWS).astype(F32)[:, None] * inv_freq[None, :]
    off = jnp.arange(PROJ_ROWS).astype(F32)[:, None] * inv_freq[None, :]
    log_gamma = jnp.log1p(-(2.0 ** (-5.0 - jnp.arange(RET_HEADS, dtype=F32))))
    idx = jnp.arange(RET_CHUNK, dtype=F32)
    rel = idx[:, None] - idx[None, :]
    causal = rel >= 0
    intra = jnp.where(causal, jnp.exp(log_gamma[:, None, None] * jnp.where(causal, rel, 0.0)), 0.0)
    q_dec = jnp.exp(log_gamma[:, None] * (idx + 1.0))[..., None]
    k_dec = jnp.exp(log_gamma[:, None] * (RET_CHUNK - 1.0 - idx))[..., None]
    chunk_dec = jnp.exp(log_gamma * RET_CHUNK)[:, None, None]
    reps = PROJ_ROWS // RET_CHUNK
    tile = lambda d: jnp.tile(jnp.broadcast_to(d, (RET_HEADS, RET_CHUNK, half)), (1, reps, 1))
    return (jnp.stack([jnp.cos(base), jnp.sin(base)], axis=1), jnp.stack([jnp.cos(off), jnp.sin(off)]),
            intra, tile(q_dec), tile(k_dec),
            jnp.broadcast_to(chunk_dec, (RET_HEADS, 1, RET_V_DIM)))


def _ret_body(q_ref, k_ref, qd_ref, kd_ref, v_ref, intra_ref, cdec_ref, o_ref, s_ref):
    @pl.when(pl.program_id(1) == 0)
    def _():
        s_ref[...] = jnp.zeros_like(s_ref)

    for j in range(RET_STEP // RET_CHUNK):
        rows = slice(j * RET_CHUNK, (j + 1) * RET_CHUNK)
        for h in range(RET_HEADS):
            qs = slice(h * RET_QK_DIM, (h + 1) * RET_QK_DIM)
            vs = slice(h * RET_V_DIM, (h + 1) * RET_V_DIM)
            v = v_ref[rows, vs]
            s = s_ref[h]
            scores = _dot_nt(q_ref[rows, qs], k_ref[rows, qs]) * intra_ref[h]
            o = _dot(scores, v) + _dot(qd_ref[rows, qs], s)
            s_ref[h] = cdec_ref[h] * s + _dot_tn(kd_ref[rows, qs], v)
            o_ref[rows, vs] = o.astype(o_ref.dtype)


def _ret_mixer_body(x_ref, g_ref, w_ref, base_ref, off_ref, qdec_ref, kdec_ref, intra_ref, cdec_ref,
                    o_ref, gate_ref, s_ref, q_ref, k_ref, qd_ref, kd_ref, v_ref):
    _ret_proj_body(x_ref, g_ref, w_ref, base_ref, off_ref, qdec_ref, kdec_ref,
                   q_ref, k_ref, qd_ref, kd_ref, v_ref, gate_ref)
    _ret_body(q_ref, k_ref, qd_ref, kd_ref, v_ref, intra_ref, cdec_ref, o_ref, s_ref)


def _retention_mixer(x, gain, w, layer, rope_base, rope_off, q_dec, k_dec, intra, chunk_dec, batch, seq):
    n, d = x.shape
    half = RET_QK_DIM // 2
    nt = seq // PROJ_ROWS
    blk = lambda width: pl.BlockSpec((PROJ_ROWS, width), lambda b, t: (b * nt + t, 0))
    base = pl.BlockSpec((None, 2, half), lambda b, t: (t, 0, 0))
    qk_scratch = pltpu.VMEM((PROJ_ROWS, RET_QK_W), BF16)
    return pl.pallas_call(
        _ret_mixer_body,
        grid=(batch, nt),
        in_specs=[blk(d), _resident((1, d)), _layer(w, layer), base, _resident(rope_off.shape),
                  _resident(q_dec.shape), _resident(k_dec.shape),
                  _resident(intra.shape), _resident(chunk_dec.shape)],
        out_specs=[blk(RET_V_W), blk(RET_V_W)],
        out_shape=[jax.ShapeDtypeStruct((n, RET_V_W), BF16)] * 2,
        scratch_shapes=[pltpu.VMEM((RET_HEADS, RET_QK_DIM, RET_V_DIM), F32),
                        qk_scratch, qk_scratch, qk_scratch, qk_scratch,
                        pltpu.VMEM((PROJ_ROWS, RET_V_W), BF16)],
        compiler_params=pltpu.CompilerParams(dimension_semantics=("parallel", "arbitrary"),
                                             vmem_limit_bytes=VMEM_LIMIT),
        name="retention_mixer",
    )(x, gain.reshape(1, d), w, rope_base, rope_off, q_dec, k_dec, intra, chunk_dec)


def _gdn_proj_body(x_ref, g_ref, w_ref, wg_ref, cw_ref,
                   q_ref, k_ref, v_ref, z_ref, ba_ref, xbuf_ref):
    rows, pad, dh = GDN_STEP, SUBLANES, GDN_DIM

    @pl.when(pl.program_id(1) == 0)
    def _():
        xbuf_ref[0:pad, :] = jnp.zeros((pad, 3 * GDN_W), F32)

    xn = _rms(x_ref[...], g_ref[...]).astype(BF16)

    def project(part):
        ps = slice(part * GDN_W, (part + 1) * GDN_W)
        xbuf_ref[pad:, ps] = jnp.dot(xn, w_ref[:, ps], preferred_element_type=F32)

    def conv(part, out_ref):
        for cb in range(GDN_W // CONV_COLS):
            lo = part * GDN_W + cb * CONV_COLS
            cs = slice(lo, lo + CONV_COLS)
            full = xbuf_ref[:, cs]
            prev = pltpu.roll(full, 1, axis=0)
            newer = cw_ref[3:4, cs] * full + cw_ref[2:3, cs] * prev
            older = cw_ref[1:2, cs] * full + cw_ref[0:1, cs] * prev
            acc = (newer + pltpu.roll(older, 2, axis=0))[pad:]
            act = _silu(acc)
            for hh in range(CONV_COLS // dh):
                a = act[:, hh * dh:(hh + 1) * dh]
                if part == 0:
                    a = a * (lax.rsqrt(jnp.sum(a * a, axis=-1, keepdims=True) + EPS) * (dh ** -0.5))
                elif part == 1:
                    a = _l2norm(a)
                oc = cb * CONV_COLS + hh * dh
                out_ref[:, oc:oc + dh] = a.astype(BF16)
        ps = slice(part * GDN_W, (part + 1) * GDN_W)
        xbuf_ref[0:pad, ps] = xbuf_ref[rows:rows + pad, ps]

    project(0)
    project(1)
    conv(0, q_ref)
    project(2)
    conv(1, k_ref)
    z_ref[...] = jnp.dot(xn, w_ref[:, 3 * GDN_W:], preferred_element_type=F32).astype(BF16)
    ba_ref[...] = jnp.dot(xn, wg_ref[...], preferred_element_type=F32)
    conv(2, v_ref)


def _gdn_block(q_ref, k_ref, v_ref, ba_ref, gp_ref, o_ref, s_ref, qk_ref):
    tb, c, dh, nh = GDN_BLOCK, GDN_CHUNK, GDN_DIM, GDN_HEADS
    nc = tb // c

    ba = ba_ref[...]
    beta = jax.nn.sigmoid(ba)
    xs = ba + gp_ref[1:2, :]
    softplus = jnp.maximum(xs, 0.0) + jnp.log1p(jnp.exp(-jnp.abs(xs)))
    g = -jnp.exp(gp_ref[0:1, :]) * softplus
    shift = c.bit_length() - 1
    ri = lax.broadcasted_iota(jnp.int32, (tb, tb), 0)
    ci = lax.broadcasted_iota(jnp.int32, (tb, tb), 1)
    same = jnp.right_shift(ri, shift) == jnp.right_shift(ci, shift)
    causal = same & (ri >= ci)
    strict = same & (ri > ci)
    tri = jnp.where(causal, 1.0, 0.0).astype(BF16)
    g_hi = g.astype(BF16)
    g_mid = (g - g_hi.astype(F32)).astype(BF16)
    g_lo = (g - g_hi.astype(F32) - g_mid.astype(F32)).astype(BF16)
    gc = sum(jnp.dot(tri, part, preferred_element_type=F32) for part in (g_hi, g_mid, g_lo))
    gct = gc.T
    egc = jnp.exp(gc)
    eye_s = jnp.where(lax.broadcasted_iota(jnp.int32, (c, tb), 0)
                      == jnp.bitwise_and(lax.broadcasted_iota(jnp.int32, (c, tb), 1), c - 1),
                      1.0, 0.0).astype(F32)

    def fold(m):
        out = m[0:c]
        for j in range(1, nc):
            out = out + m[j * c:(j + 1) * c]
        return out

    same_bf = jnp.where(same, 1.0, 0.0).astype(BF16)
    row_s = lax.broadcasted_iota(jnp.int32, (c, tb), 0)
    col_s = jnp.bitwise_and(lax.broadcasted_iota(jnp.int32, (c, tb), 1), c - 1)

    def pair_lower(lvl):
        rb = jnp.right_shift(row_s, lvl)
        cb = jnp.right_shift(col_s, lvl)
        return (jnp.right_shift(rb, 1) == jnp.right_shift(cb, 1)) & (jnp.bitwise_and(rb, 1) == 1) \
            & (jnp.bitwise_and(cb, 1) == 0)

    def unfold(m):
        return jnp.concatenate([m.astype(BF16)] * nc, axis=0) * same_bf

    for g0 in range(0, nh, GDN_GROUP):
        heads = list(range(g0, g0 + GDN_GROUP))
        idx = range(len(heads))
        q = [q_ref[:, h * dh:(h + 1) * dh] for h in heads]
        k = [k_ref[:, h * dh:(h + 1) * dh] for h in heads]
        kf = [k[i].astype(F32) for i in idx]
        v = [v_ref[:, h * dh:(h + 1) * dh].astype(F32) for h in heads]
        bcol = [beta[:, h:h + 1] for h in heads]
        gcol = [gc[:, nh + h:nh + h + 1] for h in heads]
        ecol = [egc[:, nh + h:nh + h + 1] for h in heads]
        grow = [gct[nh + h:nh + h + 1, :] for h in heads]
        kb = [kf[i] * bcol[i] for i in idx]
        kq = [_dot_nt(jnp.concatenate([kb[i].astype(BF16), q[i]], axis=0), k[i]) for i in idx]
        decay = [jnp.exp(jnp.minimum(gcol[i] - grow[i], 0.0)) for i in idx]
        low = [jnp.where(strict, kq[i][:tb] * decay[i], 0.0) for i in idx]
        for i in idx:
            qk_ref[heads[i]] = jnp.where(causal, kq[i][tb:] * decay[i], 0.0).astype(BF16)
        low_s = [fold(low[i]) for i in idx]
        ts = [eye_s - jnp.where(pair_lower(0), low_s[i], 0.0) for i in idx]
        for lvl in range(1, shift):
            off = pair_lower(lvl)
            cx = [_dot(jnp.where(off, low_s[i], 0.0), unfold(ts[i])) for i in idx]
            ts = [ts[i] - _dot(ts[i], unfold(cx[i])) for i in idx]
        uw = [_dot(unfold(ts[i]), jnp.concatenate([v[i] * bcol[i], kb[i] * ecol[i]], axis=-1))
              for i in idx]
        qd = [q[i].astype(F32) * ecol[i] for i in idx]
        s = [s_ref[h] for h in heads]
        vnew = [[] for _ in idx]
        inter = [[] for _ in idx]
        for j in range(nc):
            r = slice(j * c, (j + 1) * c)
            ws = [_dot(jnp.concatenate([uw[i][r, dh:], qd[i][r]], axis=0), s[i]) for i in idx]
            for i in idx:
                vnew[i].append(uw[i][r, :dh] - ws[i][:c])
                inter[i].append(ws[i][c:])
            gl = [gc[(j + 1) * c - 1:(j + 1) * c, nh + h:nh + h + 1] for h in heads]
            s = [s[i] * jnp.exp(gl[i]) + _dot_tn(kf[i][r] * jnp.exp(gl[i] - gcol[i][r]), vnew[i][j])
                 for i in idx]
        for i in idx:
            h = heads[i]
            s_ref[h] = s[i]
            o = jnp.concatenate(inter[i], axis=0) + _dot(qk_ref[h], jnp.concatenate(vnew[i], axis=0))
            o_ref[:, h * dh:(h + 1) * dh] = o.astype(o_ref.dtype)


def _gdn_mixer_body(x_ref, g_ref, w_ref, wg_ref, cw_ref, gp_ref, o_ref, z_ref,
                    s_ref, qk_ref, xbuf_ref, q_ref, k_ref, v_ref, ba_ref):
    @pl.when(pl.program_id(1) == 0)
    def _():
        s_ref[...] = jnp.zeros_like(s_ref)

    _gdn_proj_body(x_ref, g_ref, w_ref, wg_ref, cw_ref, q_ref, k_ref, v_ref, z_ref, ba_ref, xbuf_ref)
    for sb in range(GDN_STEP // GDN_BLOCK):
        rows = pl.ds(sb * GDN_BLOCK, GDN_BLOCK)
        _gdn_block(q_ref.at[rows], k_ref.at[rows], v_ref.at[rows], ba_ref.at[rows], gp_ref,
                   o_ref.at[rows], s_ref, qk_ref.at[sb])


def _gdn_mixer(x, gain, w, w_gate, conv_w, layer, a_log, dt_bias, batch, seq):
    n, d = x.shape
    nt = seq // GDN_STEP
    lane_pad = GATE_LANES - 2 * GDN_HEADS
    gate_params = jnp.stack([
        jnp.pad(a_log.astype(F32), (GDN_HEADS, lane_pad)),
        jnp.pad(dt_bias.astype(F32), (GDN_HEADS, lane_pad))])
    gate_params = jnp.pad(gate_params, ((0, SUBLANES - 2), (0, 0)))
    blk = lambda width: pl.BlockSpec((GDN_STEP, width), lambda b, t: (b * nt + t, 0))
    act = pltpu.VMEM((GDN_STEP, GDN_W), BF16)
    return pl.pallas_call(
        _gdn_mixer_body,
        grid=(batch, nt),
        in_specs=[blk(d), _resident((1, d)), _layer(w, layer, 4 * GDN_W), _layer(w_gate, layer),
                  _layer(conv_w, layer), _resident((SUBLANES, GATE_LANES))],
        out_specs=[blk(GDN_W), blk(GDN_W)],
        out_shape=[jax.ShapeDtypeStruct((n, GDN_W), BF16)] * 2,
        scratch_shapes=[pltpu.VMEM((GDN_HEADS, GDN_DIM, GDN_DIM), F32),
                        pltpu.VMEM((GDN_STEP // GDN_BLOCK, GDN_HEADS, GDN_BLOCK, GDN_BLOCK), BF16),
                        pltpu.VMEM((SUBLANES + GDN_STEP, 3 * GDN_W), F32),
                        act, act, act, pltpu.VMEM((GDN_STEP, GATE_LANES), F32)],
        compiler_params=pltpu.CompilerParams(dimension_semantics=("parallel", "arbitrary"),
                                             vmem_limit_bytes=VMEM_LIMIT),
        name="gdn_mixer",
    )(x, gain.reshape(1, d), w, w_gate, conv_w, gate_params)


def _out_mlp_body(head_dim, center, o_ref, gate_ref, hg_ref, h_ref, wo_ref, gains_ref, wu_ref,
                  wd_ref, out_ref, y_ref):
    subs = [slice(r * MLP_SUB, (r + 1) * MLP_SUB) for r in range(MLP_ROWS // MLP_SUB)]
    for rs in subs:
        for hd in range(o_ref.shape[1] // head_dim):
            hs = slice(hd * head_dim, (hd + 1) * head_dim)
            o = o_ref[rs, hs].astype(F32)
            if center:
                o = o - jnp.mean(o, axis=-1, keepdims=True)
            on = o * lax.rsqrt(jnp.mean(o * o, axis=-1, keepdims=True) + EPS) * hg_ref[:, hs]
            y_ref[rs, hs] = (on * _silu(gate_ref[rs, hs].astype(F32))).astype(BF16)
    mix = [jnp.dot(y_ref[rs, :], wo_ref[...], preferred_element_type=F32) for rs in subs]
    h1 = [h_ref[rs, :] + _rms(m, gains_ref[1:2, :]) for rs, m in zip(subs, mix)]
    xn = [_rms(a, gains_ref[2:3, :]).astype(BF16) for a in h1]
    ff = [jnp.zeros_like(a) for a in h1]
    for c in range(D_FF // COL_CHUNK):
        sl = slice(c * COL_CHUNK, (c + 1) * COL_CHUNK)
        up = [jnp.dot(a, wu_ref[:, sl], preferred_element_type=F32) for a in xn]
        act = [jnp.square(jnp.maximum(u, 0.0)).astype(BF16) for u in up]
        ff = [f + jnp.dot(a, wd_ref[sl, :], preferred_element_type=F32) for f, a in zip(ff, act)]
    for rs, a, f in zip(subs, h1, ff):
        out_ref[rs, :] = a + _rms(f, gains_ref[3:4, :])


def _out_mlp(o, gate, head_gain, head_dim, center, h, w_out, mixer_layer, gains, w_up, w_down, layer):
    n, d = h.shape
    k_in = o.shape[1]
    return pl.pallas_call(
        functools.partial(_out_mlp_body, head_dim, center),
        grid=(n // MLP_ROWS,),
        in_specs=[pl.BlockSpec((MLP_ROWS, k_in), lambda i: (i, 0)),
                  pl.BlockSpec((MLP_ROWS, k_in), lambda i: (i, 0)),
                  _resident((1, k_in)),
                  pl.BlockSpec((MLP_ROWS, d), lambda i: (i, 0)),
                  _layer(w_out, mixer_layer), _layer(gains, layer),
                  _layer(w_up, layer), _layer(w_down, layer)],
        out_specs=pl.BlockSpec((MLP_ROWS, d), lambda i: (i, 0)),
        out_shape=jax.ShapeDtypeStruct((n, d), F32),
        scratch_shapes=[pltpu.VMEM((MLP_ROWS, k_in), BF16)],
        compiler_params=pltpu.CompilerParams(dimension_semantics=("parallel",),
                                             vmem_limit_bytes=VMEM_LIMIT),
        name="out_mlp",
    )(o, gate, head_gain, h, w_out, gains, w_up, w_down)


def kernel(x, norm_gains, ret_w_in, ret_gn_gain, ret_w_out, gdn_w_in, gdn_conv_w, gdn_a_log,
           gdn_dt_bias, gdn_norm_gain, gdn_w_out, mlp_w_up, mlp_w_down):
    batch, seq, d = x.shape
    h = x.reshape(batch * seq, d)
    rope_base, rope_off, intra, q_dec, k_dec, chunk_dec = _retention_tables(seq)
    gains = norm_gains.astype(F32)
    n_main = 4 * GDN_W
    ret_w_in, ret_w_out, gdn_w_out, mlp_w_up, mlp_w_down = (
        w.astype(BF16) for w in (ret_w_in, ret_w_out, gdn_w_out, mlp_w_up, mlp_w_down))
    gdn_w_main = gdn_w_in.astype(BF16)
    gdn_w_gate = jnp.pad(gdn_w_in[:, :, n_main:],
                         ((0, 0), (0, 0), (0, GATE_LANES - 2 * GDN_HEADS))).astype(BF16)
    gdn_conv_w = gdn_conv_w.astype(F32)
    for i in range(DEPTH):
        j = i // 2
        if i % 2 == 0:
            o, gate = _retention_mixer(h, gains[i, 0], ret_w_in, j, rope_base, rope_off, q_dec, k_dec,
                                       intra, chunk_dec, batch, seq)
            head_gain = ret_gn_gain[j].astype(F32).reshape(1, RET_V_W)
            h = _out_mlp(o, gate, head_gain, RET_V_DIM, True, h, ret_w_out, j, gains,
                         mlp_w_up, mlp_w_down, i)
        else:
            o, z = _gdn_mixer(h, gains[i, 0], gdn_w_main, gdn_w_gate, gdn_conv_w, j,
                              gdn_a_log[j], gdn_dt_bias[j], batch, seq)
            head_gain = jnp.tile(gdn_norm_gain[j].astype(F32), GDN_HEADS).reshape(1, GDN_W)
            h = _out_mlp(o, z, head_gain, GDN_DIM, False, h, gdn_w_out, j, gains,
                         mlp_w_up, mlp_w_down, i)
    return h.reshape(batch, seq, d)
```
